```python
import jax, jax.numpy as jnp
from jax import lax
import numpy as np

D_MODEL = 1024
BATCH = 4
SEQ = 4096
DEPTH = 2
DEC_BATCH = 32
DEC_SEQ = 16
PAST_LEN = 1024

CHUNK = 64
N_MIXERS = 2
N_RET_LAYERS = (DEPTH + 1) // 2
N_SGU_LAYERS = DEPTH // 2
RET_HEADS = 4
RET_DK = D_MODEL // RET_HEADS
RET_DV = 2 * RET_DK
RET_QK = RET_HEADS * RET_DK
RET_V = RET_HEADS * RET_DV
ROPE_BASE = 10000.0
SGU_CHUNK = 128
SGU_GROUPS = 4
SGU_D = 3 * D_MODEL
SGU_DG = SGU_D // SGU_GROUPS
FFN_D = 2816
CONV_W = 3
EPS = 1e-6

kernel_name = "hybrid_retention_sgu_convffn_stream_step"


def _rmsnorm(x, g):
    xf = x.astype(jnp.float32)
    y = xf * lax.rsqrt(jnp.mean(xf * xf, axis=-1, keepdims=True) + EPS)
    return (y * g.astype(jnp.float32)).astype(x.dtype)


def _modulate(x, g, shift, scale):
    return _rmsnorm(x, g) * (1 + scale[:, None, :]) + shift[:, None, :]


def _rotary(x, pos):
    half = x.shape[-1] // 2
    inv = jnp.power(ROPE_BASE, -jnp.arange(half, dtype=jnp.float32) / half)
    ang = pos[:, None] * inv[None, :]
    cos = jnp.cos(ang)[None, :, None, :]
    sin = jnp.sin(ang)[None, :, None, :]
    x1, x2 = x[..., :half], x[..., half:]
    return jnp.concatenate([x1 * cos - x2 * sin, x1 * sin + x2 * cos], axis=-1)


def _retention_block(S, qkv, log_gamma):
    q, k, v = qkv
    L = q.shape[1]
    idx = jnp.arange(L, dtype=jnp.float32)
    dist = jnp.abs(idx[:, None] - idx[None, :])
    intra = jnp.exp(log_gamma[:, None, None] * dist[None])
    scores = jnp.einsum('blhd,bshd->bhls', q, k) * intra[None]
    o = jnp.einsum('bhls,bshe->blhe', scores, v)
    q_dec = jnp.exp((idx[:, None] + 1.0) * log_gamma[None, :])
    o = o + jnp.einsum('blhd,bhde->blhe', q * q_dec[None, :, :, None], S)
    k_dec = jnp.exp((L - 1.0 - idx)[:, None] * log_gamma[None, :])
    S = S * jnp.exp(L * log_gamma)[None, :, None, None] + jnp.einsum(
        'blhd,blhe->bhde', k * k_dec[None, :, :, None], v)
    return S, o


def _retention(h, S0, pos0, w_in, gn_g, w_out):
    B, T, _ = h.shape
    proj = h @ w_in
    q = proj[..., :RET_QK].reshape(B, T, RET_HEADS, RET_DK).astype(jnp.float32)
    k = proj[..., RET_QK:2 * RET_QK].reshape(B, T, RET_HEADS, RET_DK).astype(jnp.float32)
    v = proj[..., 2 * RET_QK:2 * RET_QK + RET_V].reshape(B, T, RET_HEADS, RET_DV).astype(jnp.float32)
    gate = proj[..., 2 * RET_QK + RET_V:]
    pos = pos0 + jnp.arange(T, dtype=jnp.float32)
    q = _rotary(q, pos) * (RET_DK ** -0.5)
    k = _rotary(k, pos)
    L = min(T, CHUNK)
    n = T // L

    def to_blocks(a):
        return a.reshape(B, n, L, *a.shape[2:]).swapaxes(0, 1)

    log_gamma = jnp.log(1.0 - jnp.exp2(-5.0 - jnp.arange(RET_HEADS, dtype=jnp.float32)))
    S, o = lax.scan(lambda s, xs: _retention_block(s, xs, log_gamma),
                    S0.astype(jnp.float32), (to_blocks(q), to_blocks(k), to_blocks(v)))
    o = o.swapaxes(0, 1).reshape(B, T, RET_HEADS, RET_DV)
    mu = jnp.mean(o, axis=-1, keepdims=True)
    var = jnp.mean(jnp.square(o - mu), axis=-1, keepdims=True)
    o = ((o - mu) * lax.rsqrt(var + EPS)).reshape(B, T, RET_V) * gn_g.astype(jnp.float32)
    y = (jax.nn.silu(gate) * o.astype(h.dtype)) @ w_out
    return y, S


def _sgu(h, w_in, ln_g, ln_b, w_s, b_s, w_out):
    B, T, _ = h.shape
    z = jax.nn.gelu(h @ w_in)
    u, v = z[..., :SGU_D], z[..., SGU_D:]
    vf = v.astype(jnp.float32)
    mu = jnp.mean(vf, axis=-1, keepdims=True)
    var = jnp.mean(jnp.square(vf - mu), axis=-1, keepdims=True)
    v = ((vf - mu) * lax.rsqrt(var + EPS) * ln_g + ln_b).astype(h.dtype)
    L = min(T, SGU_CHUNK)
    n = T // L
    blk = jnp.arange(SGU_CHUNK) // CHUNK
    w = jnp.where(blk[None, :] <= blk[:, None], w_s, 0)[:, :L, :L]
    mixed = jnp.einsum('gps,bnsgc->bnpgc', w, v.reshape(B, n, L, SGU_GROUPS, SGU_DG))
    mixed = mixed + b_s[:, :L].T[None, None, :, :, None]
    y = (u * mixed.reshape(B, T, SGU_D)) @ w_out
    return y, v


def _conv_ffn(h, buf, w_up, conv_w, conv_b, w_down):
    a = h @ w_up
    T = a.shape[1]
    ap = jnp.concatenate([buf.astype(a.dtype), a], axis=1)
    c = conv_b
    for j in range(CONV_W):
        c = c + ap[:, j:j + T] * conv_w[j]
    gate, val = c[..., :FFN_D], c[..., FFN_D:]
    return (jax.nn.silu(gate) * val) @ w_down, ap[:, -(CONV_W - 1):]


def setup_inputs(seed: int = 0) -> dict:
    key = jax.random.key(seed)
    ks = iter(jax.random.split(key, 32))
    D = D_MODEL

    def nrm(shape, s):
        return s * jax.random.normal(next(ks), shape, jnp.float32)

    return {
        "x_prompt": nrm((BATCH, SEQ, D), 1.0),
        "x_sample": nrm((DEC_BATCH, DEC_SEQ, D), 1.0),
        "state_ret": nrm((N_RET_LAYERS, DEC_BATCH, RET_HEADS, RET_DK, RET_DV), 4.0),
        "state_ffn_conv": nrm((DEPTH, DEC_BATCH, CONV_W - 1, 2 * FFN_D), 1.0),
        "c_prompt": nrm((BATCH, D), 1.0),
        "c_sample": nrm((DEC_BATCH, D), 1.0),
        "w_ada": nrm((DEPTH, D, 6 * D), 0.5 * D ** -0.5),
        "b_ada": nrm((DEPTH, 6 * D), 0.02),
        "norm_mix_g": 1.0 + nrm((DEPTH, D), 0.02),
        "norm_ffn_g": 1.0 + nrm((DEPTH, D), 0.02),
        "ret_w_in": nrm((N_RET_LAYERS, D, 2 * RET_QK + 2 * RET_V), D ** -0.5),
        "ret_gn_g": 1.0 + nrm((N_RET_LAYERS, RET_V), 0.02),
        "ret_w_out": nrm((N_RET_LAYERS, RET_V, D), RET_V ** -0.5),
        "sgu_w_in": nrm((N_SGU_LAYERS, D, 2 * SGU_D), D ** -0.5),
        "sgu_ln_g": 1.0 + nrm((N_SGU_LAYERS, SGU_D), 0.02),
        "sgu_ln_b": nrm((N_SGU_LAYERS, SGU_D), 0.02),
        "sgu_w_s": nrm((N_SGU_LAYERS, SGU_GROUPS, SGU_CHUNK, SGU_CHUNK), 0.5 * SGU_CHUNK ** -0.5),
        "sgu_b_s": 1.0 + nrm((N_SGU_LAYERS, SGU_GROUPS, SGU_CHUNK), 0.02),
        "sgu_w_out": nrm((N_SGU_LAYERS, SGU_D, D), SGU_D ** -0.5),
        "ffn_w_up": nrm((DEPTH, D, 2 * FFN_D), D ** -0.5),
        "ffn_conv_w": nrm((DEPTH, CONV_W, 2 * FFN_D), CONV_W ** -0.5),
        "ffn_conv_b": nrm((DEPTH, 2 * FFN_D), 0.02),
        "ffn_w_down": nrm((DEPTH, FFN_D, D), FFN_D ** -0.5),
        "final_g": 1.0 + nrm((D,), 0.02),
    }


def reference(x_prompt, x_sample, state_ret, state_ffn_conv, c_prompt, c_sample,
              w_ada, b_ada, norm_mix_g, norm_ffn_g,
              ret_w_in, ret_gn_g, ret_w_out,
              sgu_w_in, sgu_ln_g, sgu_ln_b, sgu_w_s, sgu_b_s, sgu_w_out,
              ffn_w_up, ffn_conv_w, ffn_conv_b, ffn_w_down, final_g):
    xp, xs = x_prompt, x_sample
    Bp = xp.shape[0]
    ret_p, ret_s, conv_p, conv_s, sgu_s = [], [], [], [], []
    for i in range(DEPTH):
        mod_p = jax.nn.silu(c_prompt) @ w_ada[i] + b_ada[i]
        mod_s = jax.nn.silu(c_sample) @ w_ada[i] + b_ada[i]
        sh1p, sc1p, g1p, sh2p, sc2p, g2p = jnp.split(mod_p, 6, axis=-1)
        sh1s, sc1s, g1s, sh2s, sc2s, g2s = jnp.split(mod_s, 6, axis=-1)
        hp = _modulate(xp, norm_mix_g[i], sh1p, sc1p)
        hs = _modulate(xs, norm_mix_g[i], sh1s, sc1s)
        j = i // N_MIXERS
        if i % N_MIXERS == 0:
            zero_state = jnp.zeros((Bp, RET_HEADS, RET_DK, RET_DV), jnp.float32)
            op, Sp = _retention(hp, zero_state, 0, ret_w_in[j], ret_gn_g[j], ret_w_out[j])
            os_, Ss = _retention(hs, state_ret[j], PAST_LEN, ret_w_in[j], ret_gn_g[j], ret_w_out[j])
            ret_p.append(Sp.astype(xp.dtype))
            ret_s.append(Ss.astype(state_ret.dtype))
        else:
            op, _ = _sgu(hp, sgu_w_in[j], sgu_ln_g[j], sgu_ln_b[j], sgu_w_s[j], sgu_b_s[j], sgu_w_out[j])
            os_, vs = _sgu(hs, sgu_w_in[j], sgu_ln_g[j], sgu_ln_b[j], sgu_w_s[j], sgu_b_s[j], sgu_w_out[j])
            sgu_s.append(vs)
        xp = xp + g1p[:, None, :] * op
        xs = xs + g1s[:, None, :] * os_
        hp = _modulate(xp, norm_ffn_g[i], sh2p, sc2p)
        hs = _modulate(xs, norm_ffn_g[i], sh2s, sc2s)
        zero_buf = jnp.zeros((Bp, CONV_W - 1, 2 * FFN_D), hp.dtype)
        fp, bp = _conv_ffn(hp, zero_buf, ffn_w_up[i], ffn_conv_w[i], ffn_conv_b[i], ffn_w_down[i])
        fs, bs = _conv_ffn(hs, state_ffn_conv[i], ffn_w_up[i], ffn_conv_w[i], ffn_conv_b[i], ffn_w_down[i])
        conv_p.append(bp)
        conv_s.append(bs)
        xp = xp + g2p[:, None, :] * fp
        xs = xs + g2s[:, None, :] * fs
    y_prompt = _rmsnorm(xp, final_g)
    y_sample = _rmsnorm(xs, final_g)
    return (y_prompt, y_sample, jnp.stack(ret_p), jnp.stack(ret_s),
            jnp.stack(conv_p), jnp.stack(conv_s), jnp.stack(sgu_s))
```

```python
import functools
import math

import jax
import jax.numpy as jnp
from jax import lax
from jax.experimental import pallas as pl
from jax.experimental.pallas import tpu as pltpu

D_MODEL = 1024
N_MOD = 6
RET_HEADS = 4
RET_DK = 256
RET_DV = 512
RET_QK = RET_HEADS * RET_DK
RET_V = RET_HEADS * RET_DV
RET_CHUNK = 64
ROPE_BASE = 10000.0
PAST_LEN = 1024
SGU_CHUNK = 128
SGU_GROUPS = 4
SGU_D = 3 * D_MODEL
SGU_DG = SGU_D // SGU_GROUPS
FFN_D = 2816
CONV_W = 3
EPS = 1e-6

ROW_TILE = 256
MIB = 1024 * 1024

BF16 = jnp.bfloat16
F32 = jnp.float32


def _dot(a, b):
    return jnp.dot(a, b, preferred_element_type=F32)


def _resident(shape):
    nd = len(shape)
    return pl.BlockSpec(shape, lambda *_: (0,) * nd, pipeline_mode=pl.Buffered(1))


def _params(vmem_mib):
    return pltpu.CompilerParams(
        dimension_semantics=("arbitrary", "arbitrary"),
        vmem_limit_bytes=vmem_mib * MIB)


def _rows(v, s, t):
    if s == 1:
        return v
    return jnp.broadcast_to(v[:, None, :], (s, t, v.shape[-1])).reshape(s * t, v.shape[-1])


def _rmsnorm(x, g):
    ms = jnp.mean(x * x, axis=-1, keepdims=True)
    return x * lax.rsqrt(ms + EPS) * g


def _modulated(x, g, shift, scale, s, t):
    return _rmsnorm(x, g) * (1.0 + _rows(scale, s, t)) + _rows(shift, s, t)


def _silu(x):
    return x * jax.nn.sigmoid(x)


def _layernorm_nogain(o):
    mu = jnp.mean(o, axis=-1, keepdims=True)
    d = o - mu
    var = jnp.mean(d * d, axis=-1, keepdims=True)
    return d * lax.rsqrt(var + EPS)


def _ada_body(c_ref, w_ref, b_ref, o_ref):
    a = _silu(c_ref[...]).astype(BF16)
    o_ref[...] = _dot(a, w_ref[...].astype(BF16)) + b_ref[...]


def _ada(c_all, w_ada, b_ada):
    depth = w_ada.shape[0]
    n = c_all.shape[0]
    return pl.pallas_call(
        _ada_body,
        grid=(depth, N_MOD),
        in_specs=[
            pl.BlockSpec((n, D_MODEL), lambda i, j: (0, 0)),
            pl.BlockSpec((None, D_MODEL, D_MODEL), lambda i, j: (i, 0, j)),
            pl.BlockSpec((None, 1, D_MODEL), lambda i, j: (i, 0, j)),
        ],
        out_specs=pl.BlockSpec((None, n, D_MODEL), lambda i, j: (i, 0, j)),
        out_shape=jax.ShapeDtypeStruct((depth, n, N_MOD * D_MODEL), F32),
        compiler_params=_params(32),
        name="ada",
    )(c_all, w_ada, b_ada.reshape(depth, 1, N_MOD * D_MODEL))


def _rotary(x, cos, sin):
    half = x.shape[-1] // 2
    x1, x2 = x[:, :half], x[:, half:]
    return jnp.concatenate([x1 * cos - x2 * sin, x1 * sin + x2 * cos], axis=-1)


def _ret_head(q, k, v, s_prev, cos, sin, mask, qdec, kdec, gs):
    qb = (_rotary(q, cos, sin) * (RET_DK ** -0.5)).astype(BF16)
    kr = _rotary(k, cos, sin)
    kb = kr.astype(BF16)
    vb = v.astype(BF16)
    scores = lax.dot_general(qb, kb, (((1,), (1,)), ((), ())),
                             preferred_element_type=F32) * mask
    o = _dot(scores.astype(BF16), vb) + qdec * _dot(qb, s_prev.astype(BF16))
    kd = (kr * kdec).T.astype(BF16)
    s_new = gs * s_prev + _dot(kd, vb)
    return o, s_new


def _ret_prompt_body(x_ref, mod_ref, g_ref, win_ref, cos_ref, sin_ref, mask_ref,
                     qdec_ref, kdec_ref, gs_ref, gng_ref, wout_ref,
                     xo_ref, so_ref, s_ref):
    t = pl.program_id(1)

    @pl.when(t == 0)
    def _():
        s_ref[...] = jnp.zeros_like(s_ref)

    x = x_ref[...]
    mod = mod_ref[...]
    h = _modulated(x, g_ref[...], mod[:, 0:D_MODEL], mod[:, D_MODEL:2 * D_MODEL],
                   1, ROW_TILE).astype(BF16)
    cos = cos_ref[...]
    sin = sin_ref[...]
    y = jnp.zeros((ROW_TILE, D_MODEL), F32)
    for hd in range(RET_HEADS):
        q = _dot(h, win_ref[:, hd * RET_DK:(hd + 1) * RET_DK])
        k = _dot(h, win_ref[:, RET_QK + hd * RET_DK:RET_QK + (hd + 1) * RET_DK])
        v = _dot(h, win_ref[:, 2 * RET_QK + hd * RET_DV:2 * RET_QK + (hd + 1) * RET_DV])
        gate = _dot(h, win_ref[:, 2 * RET_QK + RET_V + hd * RET_DV:
                               2 * RET_QK + RET_V + (hd + 1) * RET_DV])
        o, s_new = _ret_head(q, k, v, s_ref[hd], cos, sin, mask_ref[hd],
                             qdec_ref[:, hd:hd + 1], kdec_ref[:, hd:hd + 1],
                             gs_ref[:, hd:hd + 1])
        s_ref[hd] = s_new
        on = _layernorm_nogain(o) * gng_ref[:, hd * RET_DV:(hd + 1) * RET_DV]
        gated = (_silu(gate) * on).astype(BF16)
        y = y + _dot(gated, wout_ref[hd * RET_DV:(hd + 1) * RET_DV, :])
    xo_ref[...] = x + mod[:, 2 * D_MODEL:3 * D_MODEL] * y

    @pl.when(t == pl.num_programs(1) - 1)
    def _():
        so_ref[...] = s_ref[...]


def _ret_prompt(x, mod, g, w_in, cos, sin, mask, qdec, kdec, gs, gn_g, w_out):
    b, seq, _ = x.shape
    nt = seq // ROW_TILE
    return pl.pallas_call(
        _ret_prompt_body,
        grid=(b, nt),
        in_specs=[
            pl.BlockSpec((None, ROW_TILE, D_MODEL), lambda i, t: (i, t, 0)),
            pl.BlockSpec((None, 1, N_MOD * D_MODEL), lambda i, t: (i, 0, 0)),
            _resident((1, D_MODEL)),
            _resident(w_in.shape),
            pl.BlockSpec((ROW_TILE, RET_DK // 2), lambda i, t: (t, 0)),
            pl.BlockSpec((ROW_TILE, RET_DK // 2), lambda i, t: (t, 0)),
            _resident(mask.shape),
            _resident(qdec.shape),
            _resident(kdec.shape),
            _resident(gs.shape),
            _resident((1, RET_V)),
            _resident(w_out.shape),
        ],
        out_specs=[
            pl.BlockSpec((None, ROW_TILE, D_MODEL), lambda i, t: (i, t, 0)),
            pl.BlockSpec((None, RET_HEADS, RET_DK, RET_DV), lambda i, t: (i, 0, 0, 0)),
        ],
        out_shape=[
            jax.ShapeDtypeStruct(x.shape, F32),
            jax.ShapeDtypeStruct((b, RET_HEADS, RET_DK, RET_DV), F32),
        ],
        scratch_shapes=[pltpu.VMEM((RET_HEADS, RET_DK, RET_DV), F32)],
        compiler_params=_params(56),
        name="ret_prompt",
    )(x, mod.reshape(b, 1, -1), g, w_in, cos, sin, mask, qdec, kdec, gs, gn_g, w_out)


def _proj_body(s, t, x_ref, mod_ref, g_ref, w_ref, o_ref):
    mod = mod_ref[...]
    h = _modulated(x_ref[...], g_ref[...], mod[:, 0:D_MODEL],
                   mod[:, D_MODEL:2 * D_MODEL], s, t).astype(BF16)
    o_ref[...] = _dot(h, w_ref[...])


def _proj_sample(x, mod, g, w, s, t):
    nb, tm, _ = x.shape
    n = w.shape[1]
    return pl.pallas_call(
        functools.partial(_proj_body, s, t),
        grid=(nb, 1),
        in_specs=[
            pl.BlockSpec((None, tm, D_MODEL), lambda i, j: (i, 0, 0)),
            pl.BlockSpec((None, s, N_MOD * D_MODEL), lambda i, j: (i, 0, 0)),
            _resident((1, D_MODEL)),
            _resident(w.shape),
        ],
        out_specs=pl.BlockSpec((None, tm, n), lambda i, j: (i, 0, 0)),
        out_shape=jax.ShapeDtypeStruct((nb, tm, n), F32),
        compiler_params=_params(48),
        name="ret_sample_proj",
    )(x, mod, g, w)


def _ret_sample_core_body(p_ref, s0_ref, cos_ref, sin_ref, mask_ref, qdec_ref,
                          kdec_ref, gs_ref, gng_ref, o_ref, so_ref):
    cos = cos_ref[...]
    sin = sin_ref[...]
    for hd in range(RET_HEADS):
        q = p_ref[:, hd * RET_DK:(hd + 1) * RET_DK]
        k = p_ref[:, RET_QK + hd * RET_DK:RET_QK + (hd + 1) * RET_DK]
        v = p_ref[:, 2 * RET_QK + hd * RET_DV:2 * RET_QK + (hd + 1) * RET_DV]
        gate = p_ref[:, 2 * RET_QK + RET_V + hd * RET_DV:
                     2 * RET_QK + RET_V + (hd + 1) * RET_DV]
        o, s_new = _ret_head(q, k, v, s0_ref[hd], cos, sin, mask_ref[hd],
                             qdec_ref[:, hd:hd + 1], kdec_ref[:, hd:hd + 1],
                             gs_ref[:, hd:hd + 1])
        so_ref[hd] = s_new
        on = _layernorm_nogain(o) * gng_ref[:, hd * RET_DV:(hd + 1) * RET_DV]
        o_ref[:, hd * RET_DV:(hd + 1) * RET_DV] = _silu(gate) * on


def _ret_sample_core(proj, s0, cos, sin, mask, qdec, kdec, gs, gn_g):
    b, t, n = proj.shape
    return pl.pallas_call(
        _ret_sample_core_body,
        grid=(b, 1),
        in_specs=[
            pl.BlockSpec((None, t, n), lambda i, j: (i, 0, 0)),
            pl.BlockSpec((None, RET_HEADS, RET_DK, RET_DV), lambda i, j: (i, 0, 0, 0)),
            _resident(cos.shape),
            _resident(sin.shape),
            _resident(mask.shape),
            _resident(qdec.shape),
            _resident(kdec.shape),
            _resident(gs.shape),
            _resident((1, RET_V)),
        ],
        out_specs=[
            pl.BlockSpec((None, t, RET_V), lambda i, j: (i, 0, 0)),
            pl.BlockSpec((None, RET_HEADS, RET_DK, RET_DV), lambda i, j: (i, 0, 0, 0)),
        ],
        out_shape=[
            jax.ShapeDtypeStruct((b, t, RET_V), F32),
            jax.ShapeDtypeStruct(s0.shape, F32),
        ],
        compiler_params=_params(32),
        name="ret_sample_core",
    )(proj, s0, cos, sin, mask, qdec, kdec, gs, gn_g)


def _out_body(s, t, a_ref, x_ref, mod_ref, w_ref, o_ref):
    y = _dot(a_ref[...].astype(BF16), w_ref[...])
    gate = _rows(mod_ref[...][:, 2 * D_MODEL:3 * D_MODEL], s, t)
    o_ref[...] = x_ref[...] + gate * y


def _out_sample(a, x, mod, w, s, t):
    nb, tm, k = a.shape
    return pl.pallas_call(
        functools.partial(_out_body, s, t),
        grid=(nb, 1),
        in_specs=[
            pl.BlockSpec((None, tm, k), lambda i, j: (i, 0, 0)),
            pl.BlockSpec((None, tm, D_MODEL), lambda i, j: (i, 0, 0)),
            pl.BlockSpec((None, s, N_MOD * D_MODEL), lambda i, j: (i, 0, 0)),
            _resident(w.shape),
        ],
        out_specs=pl.BlockSpec((None, tm, D_MODEL), lambda i, j: (i, 0, 0)),
        out_shape=jax.ShapeDtypeStruct(x.shape, F32),
        compiler_params=_params(32),
        name="ret_sample_out",
    )(a, x, mod, w)


def _decay_tables(block, chunk):
    log_gamma = jnp.log(1.0 - jnp.exp2(-5.0 - jnp.arange(RET_HEADS, dtype=F32)))
    idx = jnp.arange(block, dtype=F32)
    ch = jnp.arange(block) // chunk
    dist = jnp.abs(idx[:, None] - idx[None, :])
    decay = jnp.exp(log_gamma[:, None, None] * dist[None])
    mask = jnp.where((ch[None, :] <= ch[:, None])[None], decay, 0.0)
    qdec = jnp.exp((idx[:, None] + 1.0) * log_gamma[None, :])
    kdec = jnp.exp((block - 1.0 - idx)[:, None] * log_gamma[None, :])
    gs = jnp.exp(block * log_gamma)[None, :]
    return mask, qdec, kdec, gs


def _rope_tables(pos):
    half = RET_DK // 2
    inv = jnp.power(ROPE_BASE, -jnp.arange(half, dtype=F32) / half)
    ang = pos[:, None] * inv[None, :]
    return jnp.cos(ang), jnp.sin(ang)


def _sgu_body(s, t, emit_v, x_ref, mod_ref, g_ref, win_ref, lng_ref, lnb_ref,
              wmix_ref, bmix_ref, wout_ref, xo_ref, *maybe_v_ref):
    x = x_ref[...]
    mod = mod_ref[...]
    h = _modulated(x, g_ref[...], mod[:, 0:D_MODEL], mod[:, D_MODEL:2 * D_MODEL],
                   s, t).astype(BF16)
    u = jax.nn.gelu(_dot(h, win_ref[:, :SGU_D]))
    v = jax.nn.gelu(_dot(h, win_ref[:, SGU_D:]))
    v = _layernorm_nogain(v) * lng_ref[...] + lnb_ref[...]
    if emit_v:
        maybe_v_ref[0][...] = v
    vb = v.astype(BF16)
    mixed = []
    for grp in range(SGU_GROUPS):
        m = _dot(wmix_ref[grp], vb[:, grp * SGU_DG:(grp + 1) * SGU_DG])
        mixed.append(m + bmix_ref[:, grp:grp + 1])
    gated = (u * jnp.concatenate(mixed, axis=-1)).astype(BF16)
    y = _dot(gated, wout_ref[...])
    xo_ref[...] = x + _rows(mod[:, 2 * D_MODEL:3 * D_MODEL], s, t) * y


def _sgu(x, mod, g, w_in, ln_g, ln_b, w_mix, b_mix, w_out, s, t, emit_v):
    nb, r, _ = x.shape
    tm = s * t
    out_specs = [pl.BlockSpec((None, tm, D_MODEL), lambda i, j: (i, j, 0))]
    out_shape = [jax.ShapeDtypeStruct(x.shape, F32)]
    if emit_v:
        out_specs.append(pl.BlockSpec((None, tm, SGU_D), lambda i, j: (i, j, 0)))
        out_shape.append(jax.ShapeDtypeStruct((nb, r, SGU_D), F32))
    return pl.pallas_call(
        functools.partial(_sgu_body, s, t, emit_v),
        grid=(nb, r // tm),
        in_specs=[
            pl.BlockSpec((None, tm, D_MODEL), lambda i, j: (i, j, 0)),
            pl.BlockSpec((None, s, N_MOD * D_MODEL), lambda i, j: (i, 0, 0)),
            _resident((1, D_MODEL)),
            _resident(w_in.shape),
            _resident((1, SGU_D)),
            _resident((1, SGU_D)),
            _resident(w_mix.shape),
            _resident(b_mix.shape),
            _resident(w_out.shape),
        ],
        out_specs=out_specs,
        out_shape=out_shape,
        compiler_params=_params(56),
        name="sgu_sample" if emit_v else "sgu_prompt",
    )(x, mod, g, w_in, ln_g, ln_b, w_mix, b_mix, w_out)


def _sgu_mix_tables(w_s, b_s, seq_len):
    length = min(seq_len, SGU_CHUNK)
    blk = jnp.arange(SGU_CHUNK) // RET_CHUNK
    w = jnp.where(blk[None, :] <= blk[:, None], w_s, 0)[:, :length, :length]
    reps = ROW_TILE // length
    tile = jnp.zeros((SGU_GROUPS, ROW_TILE, ROW_TILE), w.dtype)
    for r in range(reps):
        tile = tile.at[:, r * length:(r + 1) * length, r * length:(r + 1) * length].set(w)
    bias = jnp.tile(b_s[:, :length].T, (reps, 1))
    return tile.astype(BF16), bias


def _ffn_body(s, t, final_norm, x_ref, mod_ref, g_ref, wup_ref, cw_ref, cb_ref,
              wdn_ref, buf_ref, fg_ref, xo_ref, co_ref, carry_ref):
    tm = s * t
    x = x_ref[...]
    mod = mod_ref[...]
    h = _modulated(x, g_ref[...], mod[:, 3 * D_MODEL:4 * D_MODEL],
                   mod[:, 4 * D_MODEL:5 * D_MODEL], s, t).astype(BF16)
    a = _dot(h, wup_ref[...])
    row = lax.broadcasted_iota(jnp.int32, (tm, 1), 0)
    r1 = pltpu.roll(a, 1, 0)
    r2 = pltpu.roll(a, 2, 0)
    if s == 1:
        @pl.when(pl.program_id(1) == 0)
        def _():
            carry_ref[...] = buf_ref[0]

        prev = carry_ref[...]
        p0, p1 = prev[0:1, :], prev[1:2, :]
        last2 = a[tm - 2:, :]
        carry_ref[...] = last2
        co_ref[0] = last2
        pos = row
    else:
        prev = buf_ref[...]
        p0 = _rows(prev[:, 0, :], s, t)
        p1 = _rows(prev[:, 1, :], s, t)
        co_ref[...] = a.reshape(s, t, 2 * FFN_D)[:, t - 2:, :]
        pos = row & (t - 1)
    a1 = jnp.where(pos == 0, p1, r1)
    a2 = jnp.where(pos == 0, p0, jnp.where(pos == 1, p1, r2))
    c = cb_ref[...] + a2 * cw_ref[0:1, :]
    c = c + a1 * cw_ref[1:2, :]
    c = c + a * cw_ref[2:3, :]
    hid = (_silu(c[:, :FFN_D]) * c[:, FFN_D:]).astype(BF16)
    y = _dot(hid, wdn_ref[...])
    xn = x + _rows(mod[:, 5 * D_MODEL:6 * D_MODEL], s, t) * y
    if final_norm:
        xn = _rmsnorm(xn, fg_ref[...])
    xo_ref[...] = xn


def _ffn(x, mod, g, w_up, conv_w, conv_b, w_down, buf, final_g, s, t, final_norm, name):
    nb, r, _ = x.shape
    tm = s * t
    assert t & (t - 1) == 0 and t >= CONV_W - 1
    return pl.pallas_call(
        functools.partial(_ffn_body, s, t, final_norm),
        grid=(nb, r // tm),
        in_specs=[
            pl.BlockSpec((None, tm, D_MODEL), lambda i, j: (i, j, 0)),
            pl.BlockSpec((None, s, N_MOD * D_MODEL), lambda i, j: (i, 0, 0)),
            _resident((1, D_MODEL)),
            _resident(w_up.shape),
            _resident((CONV_W, 2 * FFN_D)),
            _resident((1, 2 * FFN_D)),
            _resident(w_down.shape),
            pl.BlockSpec((None, s, CONV_W - 1, 2 * FFN_D), lambda i, j: (i, 0, 0, 0)),
            _resident((1, D_MODEL)),
        ],
        out_specs=[
            pl.BlockSpec((None, tm, D_MODEL), lambda i, j: (i, j, 0)),
            pl.BlockSpec((None, s, CONV_W - 1, 2 * FFN_D), lambda i, j: (i, 0, 0, 0)),
        ],
        out_shape=[
            jax.ShapeDtypeStruct(x.shape, F32),
            jax.ShapeDtypeStruct(buf.shape, F32),
        ],
        scratch_shapes=[pltpu.VMEM((CONV_W - 1, 2 * FFN_D), F32)],
        compiler_params=_params(56),
        name=name,
    )(x, mod, g, w_up, conv_w, conv_b, w_down, buf, final_g)


def kernel(x_prompt, x_sample, state_ret, state_ffn_conv, c_prompt, c_sample,
           w_ada, b_ada, norm_mix_g, norm_ffn_g,
           ret_w_in, ret_gn_g, ret_w_out,
           sgu_w_in, sgu_ln_g, sgu_ln_b, sgu_w_s, sgu_b_s, sgu_w_out,
           ffn_w_up, ffn_conv_w, ffn_conv_b, ffn_w_down, final_g):
    bp, seq, _ = x_prompt.shape
    bs, dec, _ = x_sample.shape
    depth = w_ada.shape[0]
    ss = ROW_TILE // dec
    nbs = bs // ss

    mod = _ada(jnp.concatenate([c_prompt, c_sample], axis=0), w_ada, b_ada)
    mod_p = mod[:, :bp].reshape(depth, bp, 1, -1)
    mod_s = mod[:, bp:].reshape(depth, nbs, ss, -1)

    xp = x_prompt
    xs = x_sample.reshape(nbs, ROW_TILE, D_MODEL)
    fg = final_g.reshape(1, D_MODEL)
    zero_buf = jnp.zeros((bp, 1, CONV_W - 1, 2 * FFN_D), F32)

    ret_p, ret_s, conv_p, conv_s, sgu_s = [], [], [], [], []
    for i in range(depth):
        j = i // 2
        g_mix = norm_mix_g[i].reshape(1, D_MODEL)
        if i % 2 == 0:
            w_in = ret_w_in[j].astype(BF16)
            w_out = ret_w_out[j].astype(BF16)
            gn_g = ret_gn_g[j].reshape(1, RET_V)
            cos_p, sin_p = _rope_tables(jnp.arange(seq, dtype=F32))
            xp, sp = _ret_prompt(xp, mod_p[i], g_mix, w_in, cos_p, sin_p,
                                 *_decay_tables(ROW_TILE, RET_CHUNK), gn_g, w_out)
            cos_s, sin_s = _rope_tables(PAST_LEN + jnp.arange(dec, dtype=F32))
            proj = _proj_sample(xs, mod_s[i], g_mix, w_in, ss, dec)
            gated, s_new = _ret_sample_core(
                proj.reshape(bs, dec, -1), state_ret[j], cos_s, sin_s,
                *_decay_tables(dec, min(dec, RET_CHUNK)), gn_g)
            xs = _out_sample(gated.reshape(nbs, ROW_TILE, RET_V), xs, mod_s[i],
                             w_out, ss, dec)
            ret_p.append(sp)
            ret_s.append(s_new)
        else:
            w_in = sgu_w_in[j].astype(BF16)
            w_out = sgu_w_out[j].astype(BF16)
            ln_g = sgu_ln_g[j].reshape(1, SGU_D)
            ln_b = sgu_ln_b[j].reshape(1, SGU_D)
            wm_p, bm_p = _sgu_mix_tables(sgu_w_s[j], sgu_b_s[j], seq)
            wm_s, bm_s = _sgu_mix_tables(sgu_w_s[j], sgu_b_s[j], dec)
            (xp,) = _sgu(xp, mod_p[i], g_mix, w_in, ln_g, ln_b, wm_p, bm_p, w_out,
                         1, ROW_TILE, False)
            xs, vs = _sgu(xs, mod_s[i], g_mix, w_in, ln_g, ln_b, wm_s, bm_s, w_out,
                          ss, dec, True)
            sgu_s.append(vs.reshape(bs, dec, SGU_D))
        g_ffn = norm_ffn_g[i].reshape(1, D_MODEL)
        w_up = ffn_w_up[i].astype(BF16)
        w_down = ffn_w_down[i].astype(BF16)
        conv_b = ffn_conv_b[i].reshape(1, 2 * FFN_D)
        last = i == depth - 1
        xp, cp = _ffn(xp, mod_p[i], g_ffn, w_up, ffn_conv_w[i], conv_b, w_down,
                      zero_buf, fg, 1, ROW_TILE, last, "ffn_prompt_%d" % i)
        xs, cs = _ffn(xs, mod_s[i], g_ffn, w_up, ffn_conv_w[i], conv_b, w_down,
                      state_ffn_conv[i].reshape(nbs, ss, CONV_W - 1, 2 * FFN_D),
                      fg, ss, dec, last, "ffn_sample_%d" % i)
        conv_p.append(cp.reshape(bp, CONV_W - 1, 2 * FFN_D))
        conv_s.append(cs.reshape(bs, CONV_W - 1, 2 * FFN_D))

    return (xp, xs.reshape(bs, dec, D_MODEL), jnp.stack(ret_p), jnp.stack(ret_s),
            jnp.stack(conv_p), jnp.stack(conv_s), jnp.stack(sgu_s))
```

```python
import functools

import jax
import jax.numpy as jnp
from jax import lax
from jax.experimental import pallas as pl
from jax.experimental.pallas import tpu as pltpu

D_MODEL = 1024
N_MOD = 6
RET_HEADS = 4
RET_DK = 256
RET_DV = 512
RET_QK = RET_HEADS * RET_DK
RET_V = RET_HEADS * RET_DV
RET_CHUNK = 64
ROPE_BASE = 10000.0
PAST_LEN = 1024
SGU_CHUNK = 128
SGU_GROUPS = 4
SGU_D = 3 * D_MODEL
SGU_DG = SGU_D // SGU_GROUPS
FFN_D = 2816
CONV_W = 3
EPS = 1e-6

ROW_TILE = 256
FFN_CHUNKS = ((0, 1024), (1024, 1024), (2048, 768))
MIB = 1024 * 1024

BF16 = jnp.bfloat16
F32 = jnp.float32


def _dot(a, b):
    return jnp.dot(a, b, preferred_element_type=F32)


def _resident(shape):
    nd = len(shape)
    return pl.BlockSpec(shape, lambda *_: (0,) * nd, pipeline_mode=pl.Buffered(1))


def _params(vmem_mib):
    return pltpu.CompilerParams(
        dimension_semantics=("arbitrary", "arbitrary"),
        vmem_limit_bytes=vmem_mib * MIB)


def _rows(v, s, t):
    if s == 1:
        return v
    return jnp.broadcast_to(v[:, None, :], (s, t, v.shape[-1])).reshape(s * t, v.shape[-1])


def _rmsnorm(x, g):
    ms = jnp.mean(x * x, axis=-1, keepdims=True)
    return x * lax.rsqrt(ms + EPS) * g


def _modulated(x, g, shift, scale, s, t):
    return _rmsnorm(x, g) * (1.0 + _rows(scale, s, t)) + _rows(shift, s, t)


def _silu(x):
    return x * jax.nn.sigmoid(x)


def _gelu(x):
    c = 2.0 * (2.0 / jnp.pi) ** 0.5
    return x * jax.nn.sigmoid(x * (c + (c * 0.044715) * (x * x)))


def _layernorm_nogain(o):
    mu = jnp.mean(o, axis=-1, keepdims=True)
    d = o - mu
    var = jnp.mean(d * d, axis=-1, keepdims=True)
    return d * lax.rsqrt(var + EPS)


def _ada_body(c_ref, w_ref, b_ref, o_ref):
    a = _silu(c_ref[...]).astype(BF16)
    o_ref[...] = _dot(a, w_ref[...].astype(BF16)) + b_ref[...]


def _ada(c_all, w_ada, b_ada):
    depth = w_ada.shape[0]
    n = c_all.shape[0]
    return pl.pallas_call(
        _ada_body,
        grid=(depth, N_MOD),
        in_specs=[
            pl.BlockSpec((n, D_MODEL), lambda i, j: (0, 0)),
            pl.BlockSpec((None, D_MODEL, D_MODEL), lambda i, j: (i, 0, j)),
            pl.BlockSpec((None, 1, D_MODEL), lambda i, j: (i, 0, j)),
        ],
        out_specs=pl.BlockSpec((None, n, D_MODEL), lambda i, j: (i, 0, j)),
        out_shape=jax.ShapeDtypeStruct((depth, n, N_MOD * D_MODEL), F32),
        compiler_params=_params(32),
        name="ada",
    )(c_all, w_ada, b_ada.reshape(depth, 1, N_MOD * D_MODEL))


def _rotary(x, cos, sin):
    half = x.shape[-1] // 2
    x1, x2 = x[:, :half], x[:, half:]
    return jnp.concatenate([x1 * cos - x2 * sin, x1 * sin + x2 * cos], axis=-1)


def _ret_head(q, k, v, s_prev, cos, sin, mask, qdec, kdec, gs):
    qb = (_rotary(q, cos, sin) * (RET_DK ** -0.5)).astype(BF16)
    kr = _rotary(k, cos, sin)
    kb = kr.astype(BF16)
    vb = v.astype(BF16)
    scores = lax.dot_general(qb, kb, (((1,), (1,)), ((), ())),
                             preferred_element_type=F32) * mask
    o = _dot(scores.astype(BF16), vb) + qdec * _dot(qb, s_prev.astype(BF16))
    kd = (kr * kdec).T.astype(BF16)
    s_new = gs * s_prev + _dot(kd, vb)
    return o, s_new


def _ret_prompt_body(x_ref, mod_ref, g_ref, win_ref, cos_ref, sin_ref, mask_ref,
                     qdec_ref, kdec_ref, gs_ref, gng_ref, wout_ref,
                     xo_ref, so_ref, s_ref):
    t = pl.program_id(1)

    @pl.when(t == 0)
    def _():
        s_ref[...] = jnp.zeros_like(s_ref)

    x = x_ref[...]
    mod = mod_ref[...]
    h = _modulated(x, g_ref[...], mod[:, 0:D_MODEL], mod[:, D_MODEL:2 * D_MODEL],
                   1, ROW_TILE).astype(BF16)
    cos = cos_ref[...]
    sin = sin_ref[...]
    y = jnp.zeros((ROW_TILE, D_MODEL), F32)
    for hd in range(RET_HEADS):
        q = _dot(h, win_ref[:, hd * RET_DK:(hd + 1) * RET_DK])
        k = _dot(h, win_ref[:, RET_QK + hd * RET_DK:RET_QK + (hd + 1) * RET_DK])
        v = _dot(h, win_ref[:, 2 * RET_QK + hd * RET_DV:2 * RET_QK + (hd + 1) * RET_DV])
        gate = _dot(h, win_ref[:, 2 * RET_QK + RET_V + hd * RET_DV:
                               2 * RET_QK + RET_V + (hd + 1) * RET_DV])
        o, s_new = _ret_head(q, k, v, s_ref[hd], cos, sin, mask_ref[hd],
                             qdec_ref[:, hd:hd + 1], kdec_ref[:, hd:hd + 1],
                             gs_ref[:, hd:hd + 1])
        s_ref[hd] = s_new
        on = _layernorm_nogain(o) * gng_ref[:, hd * RET_DV:(hd + 1) * RET_DV]
        gated = (_silu(gate) * on).astype(BF16)
        y = y + _dot(gated, wout_ref[hd * RET_DV:(hd + 1) * RET_DV, :])
    xo_ref[...] = x + mod[:, 2 * D_MODEL:3 * D_MODEL] * y

    @pl.when(t == pl.num_programs(1) - 1)
    def _():
        so_ref[...] = s_ref[...]


def _ret_prompt(x, mod, g, w_in, cos, sin, mask, qdec, kdec, gs, gn_g, w_out):
    b, seq, _ = x.shape
    nt = seq // ROW_TILE
    return pl.pallas_call(
        _ret_prompt_body,
        grid=(b, nt),
        in_specs=[
            pl.BlockSpec((None, ROW_TILE, D_MODEL), lambda i, t: (i, t, 0)),
            pl.BlockSpec((None, 1, N_MOD * D_MODEL), lambda i, t: (i, 0, 0)),
            _resident((1, D_MODEL)),
            _resident(w_in.shape),
            pl.BlockSpec((ROW_TILE, RET_DK // 2), lambda i, t: (t, 0)),
            pl.BlockSpec((ROW_TILE, RET_DK // 2), lambda i, t: (t, 0)),
            _resident(mask.shape),
            _resident(qdec.shape),
            _resident(kdec.shape),
            _resident(gs.shape),
            _resident((1, RET_V)),
            _resident(w_out.shape),
        ],
        out_specs=[
            pl.BlockSpec((None, ROW_TILE, D_MODEL), lambda i, t: (i, t, 0)),
            pl.BlockSpec((None, RET_HEADS, RET_DK, RET_DV), lambda i, t: (i, 0, 0, 0)),
        ],
        out_shape=[
            jax.ShapeDtypeStruct(x.shape, F32),
            jax.ShapeDtypeStruct((b, RET_HEADS, RET_DK, RET_DV), F32),
        ],
        scratch_shapes=[pltpu.VMEM((RET_HEADS, RET_DK, RET_DV), F32)],
        compiler_params=_params(56),
        name="ret_prompt",
    )(x, mod.reshape(b, 1, -1), g, w_in, cos, sin, mask, qdec, kdec, gs, gn_g, w_out)


def _proj_body(s, t, x_ref, mod_ref, g_ref, w_ref, o_ref):
    mod = mod_ref[...]
    h = _modulated(x_ref[...], g_ref[...], mod[:, 0:D_MODEL],
                   mod[:, D_MODEL:2 * D_MODEL], s, t).astype(BF16)
    o_ref[...] = _dot(h, w_ref[...])


def _proj_sample(x, mod, g, w, s, t):
    nb, tm, _ = x.shape
    n = w.shape[1]
    return pl.pallas_call(
        functools.partial(_proj_body, s, t),
        grid=(nb, 1),
        in_specs=[
            pl.BlockSpec((None, tm, D_MODEL), lambda i, j: (i, 0, 0)),
            pl.BlockSpec((None, s, N_MOD * D_MODEL), lambda i, j: (i, 0, 0)),
            _resident((1, D_MODEL)),
            _resident(w.shape),
        ],
        out_specs=pl.BlockSpec((None, tm, n), lambda i, j: (i, 0, 0)),
        out_shape=jax.ShapeDtypeStruct((nb, tm, n), F32),
        compiler_params=_params(48),
        name="ret_sample_proj",
    )(x, mod, g, w)


def _ret_sample_core_body(p_ref, s0_ref, cos_ref, sin_ref, mask_ref, qdec_ref,
                          kdec_ref, gs_ref, gng_ref, o_ref, so_ref):
    cos = cos_ref[...]
    sin = sin_ref[...]
    for hd in range(RET_HEADS):
        q = p_ref[:, hd * RET_DK:(hd + 1) * RET_DK]
        k = p_ref[:, RET_QK + hd * RET_DK:RET_QK + (hd + 1) * RET_DK]
        v = p_ref[:, 2 * RET_QK + hd * RET_DV:2 * RET_QK + (hd + 1) * RET_DV]
        gate = p_ref[:, 2 * RET_QK + RET_V + hd * RET_DV:
                     2 * RET_QK + RET_V + (hd + 1) * RET_DV]
        o, s_new = _ret_head(q, k, v, s0_ref[hd], cos, sin, mask_ref[hd],
                             qdec_ref[:, hd:hd + 1], kdec_ref[:, hd:hd + 1],
                             gs_ref[:, hd:hd + 1])
        so_ref[hd] = s_new
        on = _layernorm_nogain(o) * gng_ref[:, hd * RET_DV:(hd + 1) * RET_DV]
        o_ref[:, hd * RET_DV:(hd + 1) * RET_DV] = _silu(gate) * on


def _ret_sample_core(proj, s0, cos, sin, mask, qdec, kdec, gs, gn_g):
    b, t, n = proj.shape
    return pl.pallas_call(
        _ret_sample_core_body,
        grid=(b, 1),
        in_specs=[
            pl.BlockSpec((None, t, n), lambda i, j: (i, 0, 0)),
            pl.BlockSpec((None, RET_HEADS, RET_DK, RET_DV), lambda i, j: (i, 0, 0, 0)),
            _resident(cos.shape),
            _resident(sin.shape),
            _resident(mask.shape),
            _resident(qdec.shape),
            _resident(kdec.shape),
            _resident(gs.shape),
            _resident((1, RET_V)),
        ],
        out_specs=[
            pl.BlockSpec((None, t, RET_V), lambda i, j: (i, 0, 0)),
            pl.BlockSpec((None, RET_HEADS, RET_DK, RET_DV), lambda i, j: (i, 0, 0, 0)),
        ],
        out_shape=[
            jax.ShapeDtypeStruct((b, t, RET_V), F32),
            jax.ShapeDtypeStruct(s0.shape, F32),
        ],
        compiler_params=_params(32),
        name="ret_sample_core",
    )(proj, s0, cos, sin, mask, qdec, kdec, gs, gn_g)


def _out_body(s, t, a_ref, x_ref, mod_ref, w_ref, o_ref):
    y = _dot(a_ref[...].astype(BF16), w_ref[...])
    gate = _rows(mod_ref[...][:, 2 * D_MODEL:3 * D_MODEL], s, t)
    o_ref[...] = x_ref[...] + gate * y


def _out_sample(a, x, mod, w, s, t):
    nb, tm, k = a.shape
    return pl.pallas_call(
        functools.partial(_out_body, s, t),
        grid=(nb, 1),
        in_specs=[
            pl.BlockSpec((None, tm, k), lambda i, j: (i, 0, 0)),
            pl.BlockSpec((None, tm, D_MODEL), lambda i, j: (i, 0, 0)),
            pl.BlockSpec((None, s, N_MOD * D_MODEL), lambda i, j: (i, 0, 0)),
            _resident(w.shape),
        ],
        out_specs=pl.BlockSpec((None, tm, D_MODEL), lambda i, j: (i, 0, 0)),
        out_shape=jax.ShapeDtypeStruct(x.shape, F32),
        compiler_params=_params(32),
        name="ret_sample_out",
    )(a, x, mod, w)


def _decay_tables(block, chunk):
    log_gamma = jnp.log(1.0 - jnp.exp2(-5.0 - jnp.arange(RET_HEADS, dtype=F32)))
    idx = jnp.arange(block, dtype=F32)
    ch = jnp.arange(block) // chunk
    dist = jnp.abs(idx[:, None] - idx[None, :])
    decay = jnp.exp(log_gamma[:, None, None] * dist[None])
    mask = jnp.where((ch[None, :] <= ch[:, None])[None], decay, 0.0)
    qdec = jnp.exp((idx[:, None] + 1.0) * log_gamma[None, :])
    kdec = jnp.exp((block - 1.0 - idx)[:, None] * log_gamma[None, :])
    gs = jnp.exp(block * log_gamma)[None, :]
    return mask, qdec, kdec, gs


def _rope_tables(pos):
    half = RET_DK // 2
    inv = jnp.power(ROPE_BASE, -jnp.arange(half, dtype=F32) / half)
    ang = pos[:, None] * inv[None, :]
    return jnp.cos(ang), jnp.sin(ang)


def _sgu_body(s, t, emit_v, x_ref, mod_ref, g_ref, win_ref, lng_ref, lnb_ref,
              wmix_ref, bmix_ref, wout_ref, xo_ref, *maybe_v_ref):
    x = x_ref[...]
    mod = mod_ref[...]
    h = _modulated(x, g_ref[...], mod[:, 0:D_MODEL], mod[:, D_MODEL:2 * D_MODEL],
                   s, t).astype(BF16)
    groups = range(SGU_GROUPS)

    def cols(grp, base):
        return slice(base + grp * SGU_DG, base + (grp + 1) * SGU_DG)

    def proj(grp, base):
        return _dot(h, win_ref[:, cols(grp, base)])

    pv = [proj(0, SGU_D)]
    v = []
    for grp in groups:
        pv.append(proj(grp + 1, SGU_D) if grp + 1 < SGU_GROUPS else proj(0, 0))
        v.append(_gelu(pv[grp]))
    pu = [pv.pop(), proj(1, 0)]
    mu = sum(jnp.sum(vg, axis=-1, keepdims=True) for vg in v) * (1.0 / SGU_D)
    d = [vg - mu for vg in v]
    var = sum(jnp.sum(dg * dg, axis=-1, keepdims=True) for dg in d) * (1.0 / SGU_D)
    rs = lax.rsqrt(var + EPS)
    vb = []
    for grp in groups:
        vn = d[grp] * rs * lng_ref[:, cols(grp, 0)] + lnb_ref[:, cols(grp, 0)]
        if emit_v:
            maybe_v_ref[0][:, cols(grp, 0)] = vn
        vb.append(vn.astype(BF16))
    y = jnp.zeros((s * t, D_MODEL), F32)
    mixed = _dot(wmix_ref[0], vb[0])
    for grp in groups:
        if grp + 2 < SGU_GROUPS:
            pu.append(proj(grp + 2, 0))
        u = _gelu(pu[grp])
        cur = mixed
        if grp + 1 < SGU_GROUPS:
            mixed = _dot(wmix_ref[grp + 1], vb[grp + 1])
        gated = (u * (cur + bmix_ref[:, grp:grp + 1])).astype(BF16)
        y = y + _dot(gated, wout_ref[cols(grp, 0), :])
    xo_ref[...] = x + _rows(mod[:, 2 * D_MODEL:3 * D_MODEL], s, t) * y


def _sgu(x, mod, g, w_in, ln_g, ln_b, w_mix, b_mix, w_out, s, t, emit_v):
    nb, r, _ = x.shape
    tm = s * t
    out_specs = [pl.BlockSpec((None, tm, D_MODEL), lambda i, j: (i, j, 0))]
    out_shape = [jax.ShapeDtypeStruct(x.shape, F32)]
    if emit_v:
        out_specs.append(pl.BlockSpec((None, tm, SGU_D), lambda i, j: (i, j, 0)))
        out_shape.append(jax.ShapeDtypeStruct((nb, r, SGU_D), F32))
    return pl.pallas_call(
        functools.partial(_sgu_body, s, t, emit_v),
        grid=(nb, r // tm),
        in_specs=[
            pl.BlockSpec((None, tm, D_MODEL), lambda i, j: (i, j, 0)),
            pl.BlockSpec((None, s, N_MOD * D_MODEL), lambda i, j: (i, 0, 0)),
            _resident((1, D_MODEL)),
            _resident(w_in.shape),
            _resident((1, SGU_D)),
            _resident((1, SGU_D)),
            _resident(w_mix.shape),
            _resident(b_mix.shape),
            _resident(w_out.shape),
        ],
        out_specs=out_specs,
        out_shape=out_shape,
        compiler_params=_params(56),
        name="sgu_sample" if emit_v else "sgu_prompt",
    )(x, mod, g, w_in, ln_g, ln_b, w_mix, b_mix, w_out)


def _sgu_mix_tables(w_s, b_s, seq_len):
    length = min(seq_len, SGU_CHUNK)
    blk = jnp.arange(SGU_CHUNK) // RET_CHUNK
    w = jnp.where(blk[None, :] <= blk[:, None], w_s, 0)[:, :length, :length]
    reps = ROW_TILE // length
    tile = jnp.zeros((SGU_GROUPS, ROW_TILE, ROW_TILE), w.dtype)
    for r in range(reps):
        tile = tile.at[:, r * length:(r + 1) * length, r * length:(r + 1) * length].set(w)
    bias = jnp.tile(b_s[:, :length].T, (reps, 1))
    return tile.astype(BF16), bias


def _ffn_body(s, t, final_norm, x_ref, mod_ref, g_ref, wup_ref, cw_ref, cb_ref,
              wdn_ref, buf_ref, fg_ref, xo_ref, co_ref, carry_ref):
    tm = s * t
    x = x_ref[...]
    mod = mod_ref[...]
    h = _modulated(x, g_ref[...], mod[:, 3 * D_MODEL:4 * D_MODEL],
                   mod[:, 4 * D_MODEL:5 * D_MODEL], s, t).astype(BF16)
    if s == 1:
        @pl.when(pl.program_id(1) == 0)
        def _():
            carry_ref[...] = buf_ref[0]

    def up(c, w):
        return [_dot(h, wup_ref[:, base + c:base + c + w]) for base in (0, FFN_D)]

    def conv(a, lo, w):
        r1 = pltpu.roll(a, 1, 0)
        r2 = pltpu.roll(a, 2, 0)
        if s == 1:
            p0 = carry_ref[0:1, lo:lo + w]
            p1 = carry_ref[1:2, lo:lo + w]
            carry_ref[:, lo:lo + w] = a[tm - 2:, :]
            row = lax.broadcasted_iota(jnp.int32, (8, 1), 0)
            a1 = jnp.concatenate([jnp.where(row == 0, p1, r1[:8]), r1[8:]], axis=0)
            a2 = jnp.concatenate(
                [jnp.where(row == 0, p0, jnp.where(row == 1, p1, r2[:8])), r2[8:]], axis=0)
        else:
            p0 = _rows(buf_ref[:, 0, lo:lo + w], s, t)
            p1 = _rows(buf_ref[:, 1, lo:lo + w], s, t)
            co_ref[:, :, lo:lo + w] = a.reshape(s, t, w)[:, t - 2:, :]
            pos = lax.broadcasted_iota(jnp.int32, (tm, 1), 0) & (t - 1)
            a1 = jnp.where(pos == 0, p1, r1)
            a2 = jnp.where(pos == 0, p0, jnp.where(pos == 1, p1, r2))
        c = cb_ref[:, lo:lo + w] + a2 * cw_ref[0:1, lo:lo + w]
        c = c + a1 * cw_ref[1:2, lo:lo + w]
        return c + a * cw_ref[2:3, lo:lo + w]

    y = jnp.zeros((tm, D_MODEL), F32)
    nxt = up(*FFN_CHUNKS[0])
    for k, (c, w) in enumerate(FFN_CHUNKS):
        cur = nxt
        if k + 1 < len(FFN_CHUNKS):
            nxt = up(*FFN_CHUNKS[k + 1])
        gate = conv(cur[0], c, w)
        val = conv(cur[1], FFN_D + c, w)
        hid = (_silu(gate) * val).astype(BF16)
        y = y + _dot(hid, wdn_ref[c:c + w, :])
    if s == 1:
        co_ref[0] = carry_ref[...]
    xn = x + _rows(mod[:, 5 * D_MODEL:6 * D_MODEL], s, t) * y
    if final_norm:
        xn = _rmsnorm(xn, fg_ref[...])
    xo_ref[...] = xn


def _ffn(x, mod, g, w_up, conv_w, conv_b, w_down, buf, final_g, s, t, final_norm, name):
    nb, r, _ = x.shape
    tm = s * t
    assert t & (t - 1) == 0 and t >= CONV_W - 1
    return pl.pallas_call(
        functools.partial(_ffn_body, s, t, final_norm),
        grid=(nb, r // tm),
        in_specs=[
            pl.BlockSpec((None, tm, D_MODEL), lambda i, j: (i, j, 0)),
            pl.BlockSpec((None, s, N_MOD * D_MODEL), lambda i, j: (i, 0, 0)),
            _resident((1, D_MODEL)),
            _resident(w_up.shape),
            _resident((CONV_W, 2 * FFN_D)),
            _resident((1, 2 * FFN_D)),
            _resident(w_down.shape),
            pl.BlockSpec((None, s, CONV_W - 1, 2 * FFN_D), lambda i, j: (i, 0, 0, 0)),
            _resident((1, D_MODEL)),
        ],
        out_specs=[
            pl.BlockSpec((None, tm, D_MODEL), lambda i, j: (i, j, 0)),
            pl.BlockSpec((None, s, CONV_W - 1, 2 * FFN_D), lambda i, j: (i, 0, 0, 0)),
        ],
        out_shape=[
            jax.ShapeDtypeStruct(x.shape, F32),
            jax.ShapeDtypeStruct(buf.shape, F32),
        ],
        scratch_shapes=[pltpu.VMEM((CONV_W - 1, 2 * FFN_D), F32)],
        compiler_params=_params(56),
        name=name,
    )(x, mod, g, w_up, conv_w, conv_b, w_down, buf, final_g)


def kernel(x_prompt, x_sample, state_ret, state_ffn_conv, c_prompt, c_sample,
           w_ada, b_ada, norm_mix_g, norm_ffn_g,
           ret_w_in, ret_gn_g, ret_w_out,
           sgu_w_in, sgu_ln_g, sgu_ln_b, sgu_w_s, sgu_b_s, sgu_w_out,
           ffn_w_up, ffn_conv_w, ffn_conv_b, ffn_w_down, final_g):
    bp, seq, _ = x_prompt.shape
    bs, dec, _ = x_sample.shape
    depth = w_ada.shape[0]
    ss = ROW_TILE // dec
    nbs = bs // ss

    mod = _ada(jnp.concatenate([c_prompt, c_sample], axis=0), w_ada, b_ada)
    mod_p = mod[:, :bp].reshape(depth, bp, 1, -1)
    mod_s = mod[:, bp:].reshape(depth, nbs, ss, -1)

    xp = x_prompt
    xs = x_sample.reshape(nbs, ROW_TILE, D_MODEL)
    fg = final_g.reshape(1, D_MODEL)
    zero_buf = jnp.zeros((bp, 1, CONV_W - 1, 2 * FFN_D), F32)

    ret_p, ret_s, conv_p, conv_s, sgu_s = [], [], [], [], []
    for i in range(depth):
        j = i // 2
        g_mix = norm_mix_g[i].reshape(1, D_MODEL)
        if i % 2 == 0:
            w_in = ret_w_in[j].astype(BF16)
            w_out = ret_w_out[j].astype(BF16)
            gn_g = ret_gn_g[j].reshape(1, RET_V)
            cos_p, sin_p = _rope_tables(jnp.arange(seq, dtype=F32))
            xp, sp = _ret_prompt(xp, mod_p[i], g_mix, w_in, cos_p, sin_p,
                                 *_decay_tables(ROW_TILE, RET_CHUNK), gn_g, w_out)
            cos_s, sin_s = _rope_tables(PAST_LEN + jnp.arange(dec, dtype=F32))
            proj = _proj_sample(xs, mod_s[i], g_mix, w_in, ss, dec)
            gated, s_new = _ret_sample_core(
                proj.reshape(bs, dec, -1), state_ret[j], cos_s, sin_s,
                *_decay_tables(dec, min(dec, RET_CHUNK)), gn_g)
            xs = _out_sample(gated.reshape(nbs, ROW_TILE, RET_V), xs, mod_s[i],
                             w_out, ss, dec)
            ret_p.append(sp)
            ret_s.append(s_new)
        else:
            w_in = sgu_w_in[j].astype(BF16)
            w_out = sgu_w_out[j].astype(BF16)
            ln_g = sgu_ln_g[j].reshape(1, SGU_D)
            ln_b = sgu_ln_b[j].reshape(1, SGU_D)
            wm_p, bm_p = _sgu_mix_tables(sgu_w_s[j], sgu_b_s[j], seq)
            wm_s, bm_s = _sgu_mix_tables(sgu_w_s[j], sgu_b_s[j], dec)
            (xp,) = _sgu(xp, mod_p[i], g_mix, w_in, ln_g, ln_b, wm_p, bm_p, w_out,
                         1, ROW_TILE, False)
            xs, vs = _sgu(xs, mod_s[i], g_mix, w_in, ln_g, ln_b, wm_s, bm_s, w_out,
                          ss, dec, True)
            sgu_s.append(vs.reshape(bs, dec, SGU_D))
        g_ffn = norm_ffn_g[i].reshape(1, D_MODEL)
        w_up = ffn_w_up[i].astype(BF16)
        w_down = ffn_w_down[i].astype(BF16)
        conv_b = ffn_conv_b[i].reshape(1, 2 * FFN_D)
        last = i == depth - 1
        xp, cp = _ffn(xp, mod_p[i], g_ffn, w_up, ffn_conv_w[i], conv_b, w_down,
                      zero_buf, fg, 1, ROW_TILE, last, "ffn_prompt_%d" % i)
        xs, cs = _ffn(xs, mod_s[i], g_ffn, w_up, ffn_conv_w[i], conv_b, w_down,
                      state_ffn_conv[i].reshape(nbs, ss, CONV_W - 1, 2 * FFN_D),
                      fg, ss, dec, last, "ffn_sample_%d" % i)
        conv_p.append(cp.reshape(bp, CONV_W - 1, 2 * FFN_D))
        conv_s.append(cs.reshape(bs, CONV_W - 1, 2 * FFN_D))

    return (xp, xs.reshape(bs, dec, D_MODEL), jnp.stack(ret_p), jnp.stack(ret_s),
            jnp.stack(conv_p), jnp.stack(conv_s), jnp.stack(sgu_s))
```

```python
import functools

import jax
import jax.numpy as jnp
from jax import lax
from jax.experimental import pallas as pl
from jax.experimental.pallas import tpu as pltpu

D_MODEL = 1024
N_MOD = 6
RET_HEADS = 4
RET_DK = 256
RET_DV = 512
RET_QK = RET_HEADS * RET_DK
RET_V = RET_HEADS * RET_DV
RET_CHUNK = 64
ROPE_BASE = 10000.0
PAST_LEN = 1024
SGU_CHUNK = 128
SGU_GROUPS = 4
SGU_D = 3 * D_MODEL
SGU_DG = SGU_D // SGU_GROUPS
FFN_D = 2816
CONV_W = 3
EPS = 1e-6

ROW_TILE = 256
FFN_ROW_TILE = 512
FFN_CHUNKS = ((0, 1024), (1024, 1024), (2048, 768))
MIB = 1024 * 1024

BF16 = jnp.bfloat16
F32 = jnp.float32


def _dot(a, b):
    return jnp.dot(a, b, preferred_element_type=F32)


def _resident(shape, layer=None):
    nd = len(shape)
    if layer is None:
        return pl.BlockSpec(shape, lambda *_: (0,) * nd, pipeline_mode=pl.Buffered(1))
    return pl.BlockSpec((None,) + tuple(shape[1:]), lambda *_: (layer,) + (0,) * (nd - 1),
                        pipeline_mode=pl.Buffered(1))


def _params(vmem_mib):
    return pltpu.CompilerParams(
        dimension_semantics=("arbitrary", "arbitrary"),
        vmem_limit_bytes=vmem_mib * MIB)


def _rows(v, s, t):
    if s == 1:
        return v
    return jnp.broadcast_to(v[:, None, :], (s, t, v.shape[-1])).reshape(s * t, v.shape[-1])


def _rmsnorm(x, g):
    ms = jnp.mean(x * x, axis=-1, keepdims=True)
    return x * lax.rsqrt(ms + EPS) * g


def _modulated(x, g, shift, scale, s, t):
    return _rmsnorm(x, g) * (1.0 + _rows(scale, s, t)) + _rows(shift, s, t)


def _silu(x):
    return x * jax.nn.sigmoid(x)


def _gelu(x):
    c = 2.0 * (2.0 / jnp.pi) ** 0.5
    return x * jax.nn.sigmoid(x * (c + (c * 0.044715) * (x * x)))


def _layernorm_nogain(o):
    mu = jnp.mean(o, axis=-1, keepdims=True)
    d = o - mu
    var = jnp.mean(d * d, axis=-1, keepdims=True)
    return d * lax.rsqrt(var + EPS)


def _ada_body(c_ref, w_ref, b_ref, o_ref):
    a = _silu(c_ref[...]).astype(BF16)
    o_ref[...] = _dot(a, w_ref[...].astype(BF16)) + b_ref[...]


def _ada(c_all, w_ada, b_ada):
    depth = w_ada.shape[0]
    n = c_all.shape[0]
    return pl.pallas_call(
        _ada_body,
        grid=(depth, N_MOD),
        in_specs=[
            pl.BlockSpec((n, D_MODEL), lambda i, j: (0, 0)),
            pl.BlockSpec((None, D_MODEL, D_MODEL), lambda i, j: (i, 0, j)),
            pl.BlockSpec((None, 1, D_MODEL), lambda i, j: (i, 0, j)),
        ],
        out_specs=pl.BlockSpec((None, n, D_MODEL), lambda i, j: (i, 0, j)),
        out_shape=jax.ShapeDtypeStruct((depth, n, N_MOD * D_MODEL), F32),
        compiler_params=_params(32),
        name="ada",
    )(c_all, w_ada, b_ada.reshape(depth, 1, N_MOD * D_MODEL))


def _rotary(x, cos, sin):
    half = x.shape[-1] // 2
    x1, x2 = x[:, :half], x[:, half:]
    return jnp.concatenate([x1 * cos - x2 * sin, x1 * sin + x2 * cos], axis=-1)


def _ret_head(q, k, v, s_prev, cos, sin, mask, qdec, kdec, gs):
    qb = (_rotary(q, cos, sin) * (RET_DK ** -0.5)).astype(BF16)
    kr = _rotary(k, cos, sin)
    kb = kr.astype(BF16)
    vb = v.astype(BF16)
    scores = lax.dot_general(qb, kb, (((1,), (1,)), ((), ())),
                             preferred_element_type=F32) * mask
    o = _dot(scores.astype(BF16), vb) + qdec * _dot(qb, s_prev.astype(BF16))
    kd = (kr * kdec).T.astype(BF16)
    s_new = gs * s_prev + _dot(kd, vb)
    return o, s_new


def _ret_prompt_body(x_ref, mod_ref, g_ref, win_ref, cos_ref, sin_ref, mask_ref,
                     qdec_ref, kdec_ref, gs_ref, gng_ref, wout_ref,
                     xo_ref, so_ref, s_ref):
    t = pl.program_id(1)

    @pl.when(t == 0)
    def _():
        s_ref[...] = jnp.zeros_like(s_ref)

    x = x_ref[...]
    mod = mod_ref[...]
    h = _modulated(x, g_ref[...], mod[:, 0:D_MODEL], mod[:, D_MODEL:2 * D_MODEL],
                   1, ROW_TILE).astype(BF16)
    cos = cos_ref[...]
    sin = sin_ref[...]
    y = jnp.zeros((ROW_TILE, D_MODEL), F32)
    for hd in range(RET_HEADS):
        q = _dot(h, win_ref[:, hd * RET_DK:(hd + 1) * RET_DK])
        k = _dot(h, win_ref[:, RET_QK + hd * RET_DK:RET_QK + (hd + 1) * RET_DK])
        v = _dot(h, win_ref[:, 2 * RET_QK + hd * RET_DV:2 * RET_QK + (hd + 1) * RET_DV])
        gate = _dot(h, win_ref[:, 2 * RET_QK + RET_V + hd * RET_DV:
                               2 * RET_QK + RET_V + (hd + 1) * RET_DV])
        o, s_new = _ret_head(q, k, v, s_ref[hd], cos, sin, mask_ref[hd],
                             qdec_ref[:, hd:hd + 1], kdec_ref[:, hd:hd + 1],
                             gs_ref[:, hd:hd + 1])
        s_ref[hd] = s_new
        on = _layernorm_nogain(o) * gng_ref[:, hd * RET_DV:(hd + 1) * RET_DV]
        gated = (_silu(gate) * on).astype(BF16)
        y = y + _dot(gated, wout_ref[hd * RET_DV:(hd + 1) * RET_DV, :])
    xo_ref[...] = x + mod[:, 2 * D_MODEL:3 * D_MODEL] * y

    @pl.when(t == pl.num_programs(1) - 1)
    def _():
        so_ref[...] = s_ref[...]


def _ret_prompt(x, mod, g, w_in, cos, sin, mask, qdec, kdec, gs, gn_g, w_out):
    b, seq, _ = x.shape
    nt = seq // ROW_TILE
    return pl.pallas_call(
        _ret_prompt_body,
        grid=(b, nt),
        in_specs=[
            pl.BlockSpec((None, ROW_TILE, D_MODEL), lambda i, t: (i, t, 0)),
            pl.BlockSpec((None, 1, N_MOD * D_MODEL), lambda i, t: (i, 0, 0)),
            _resident((1, D_MODEL)),
            _resident(w_in.shape),
            pl.BlockSpec((ROW_TILE, RET_DK // 2), lambda i, t: (t, 0)),
            pl.BlockSpec((ROW_TILE, RET_DK // 2), lambda i, t: (t, 0)),
            _resident(mask.shape),
            _resident(qdec.shape),
            _resident(kdec.shape),
            _resident(gs.shape),
            _resident((1, RET_V)),
            _resident(w_out.shape),
        ],
        out_specs=[
            pl.BlockSpec((None, ROW_TILE, D_MODEL), lambda i, t: (i, t, 0)),
            pl.BlockSpec((None, RET_HEADS, RET_DK, RET_DV), lambda i, t: (i, 0, 0, 0)),
        ],
        out_shape=[
            jax.ShapeDtypeStruct(x.shape, F32),
            jax.ShapeDtypeStruct((b, RET_HEADS, RET_DK, RET_DV), F32),
        ],
        scratch_shapes=[pltpu.VMEM((RET_HEADS, RET_DK, RET_DV), F32)],
        compiler_params=_params(56),
        name="ret_prompt",
    )(x, mod.reshape(b, 1, -1), g, w_in, cos, sin, mask, qdec, kdec, gs, gn_g, w_out)


def _proj_body(s, t, x_ref, mod_ref, g_ref, w_ref, o_ref):
    mod = mod_ref[...]
    h = _modulated(x_ref[...], g_ref[...], mod[:, 0:D_MODEL],
                   mod[:, D_MODEL:2 * D_MODEL], s, t).astype(BF16)
    o_ref[...] = _dot(h, w_ref[...])


def _proj_sample(x, mod, g, w, s, t):
    nb, tm, _ = x.shape
    n = w.shape[1]
    return pl.pallas_call(
        functools.partial(_proj_body, s, t),
        grid=(nb, 1),
        in_specs=[
            pl.BlockSpec((None, tm, D_MODEL), lambda i, j: (i, 0, 0)),
            pl.BlockSpec((None, s, N_MOD * D_MODEL), lambda i, j: (i, 0, 0)),
            _resident((1, D_MODEL)),
            _resident(w.shape),
        ],
        out_specs=pl.BlockSpec((None, tm, n), lambda i, j: (i, 0, 0)),
        out_shape=jax.ShapeDtypeStruct((nb, tm, n), F32),
        compiler_params=_params(48),
        name="ret_sample_proj",
    )(x, mod, g, w)


def _ret_sample_core_body(p_ref, s0_ref, cos_ref, sin_ref, mask_ref, qdec_ref,
                          kdec_ref, gs_ref, gng_ref, o_ref, so_ref):
    cos = cos_ref[...]
    sin = sin_ref[...]
    for hd in range(RET_HEADS):
        q = p_ref[:, hd * RET_DK:(hd + 1) * RET_DK]
        k = p_ref[:, RET_QK + hd * RET_DK:RET_QK + (hd + 1) * RET_DK]
        v = p_ref[:, 2 * RET_QK + hd * RET_DV:2 * RET_QK + (hd + 1) * RET_DV]
        gate = p_ref[:, 2 * RET_QK + RET_V + hd * RET_DV:
                     2 * RET_QK + RET_V + (hd + 1) * RET_DV]
        o, s_new = _ret_head(q, k, v, s0_ref[hd], cos, sin, mask_ref[hd],
                             qdec_ref[:, hd:hd + 1], kdec_ref[:, hd:hd + 1],
                             gs_ref[:, hd:hd + 1])
        so_ref[hd] = s_new
        on = _layernorm_nogain(o) * gng_ref[:, hd * RET_DV:(hd + 1) * RET_DV]
        o_ref[:, hd * RET_DV:(hd + 1) * RET_DV] = _silu(gate) * on


def _ret_sample_core(proj, s0, cos, sin, mask, qdec, kdec, gs, gn_g):
    b, t, n = proj.shape
    return pl.pallas_call(
        _ret_sample_core_body,
        grid=(b, 1),
        in_specs=[
            pl.BlockSpec((None, t, n), lambda i, j: (i, 0, 0)),
            pl.BlockSpec((None, RET_HEADS, RET_DK, RET_DV), lambda i, j: (i, 0, 0, 0)),
            _resident(cos.shape),
            _resident(sin.shape),
            _resident(mask.shape),
            _resident(qdec.shape),
            _resident(kdec.shape),
            _resident(gs.shape),
            _resident((1, RET_V)),
        ],
        out_specs=[
            pl.BlockSpec((None, t, RET_V), lambda i, j: (i, 0, 0)),
            pl.BlockSpec((None, RET_HEADS, RET_DK, RET_DV), lambda i, j: (i, 0, 0, 0)),
        ],
        out_shape=[
            jax.ShapeDtypeStruct((b, t, RET_V), F32),
            jax.ShapeDtypeStruct(s0.shape, F32),
        ],
        compiler_params=_params(32),
        name="ret_sample_core",
    )(proj, s0, cos, sin, mask, qdec, kdec, gs, gn_g)


def _out_body(s, t, a_ref, x_ref, mod_ref, w_ref, o_ref):
    y = _dot(a_ref[...].astype(BF16), w_ref[...])
    gate = _rows(mod_ref[...][:, 2 * D_MODEL:3 * D_MODEL], s, t)
    o_ref[...] = x_ref[...] + gate * y


def _out_sample(a, x, mod, w, s, t):
    nb, tm, k = a.shape
    return pl.pallas_call(
        functools.partial(_out_body, s, t),
        grid=(nb, 1),
        in_specs=[
            pl.BlockSpec((None, tm, k), lambda i, j: (i, 0, 0)),
            pl.BlockSpec((None, tm, D_MODEL), lambda i, j: (i, 0, 0)),
            pl.BlockSpec((None, s, N_MOD * D_MODEL), lambda i, j: (i, 0, 0)),
            _resident(w.shape),
        ],
        out_specs=pl.BlockSpec((None, tm, D_MODEL), lambda i, j: (i, 0, 0)),
        out_shape=jax.ShapeDtypeStruct(x.shape, F32),
        compiler_params=_params(32),
        name="ret_sample_out",
    )(a, x, mod, w)


def _decay_tables(block, chunk):
    log_gamma = jnp.log(1.0 - jnp.exp2(-5.0 - jnp.arange(RET_HEADS, dtype=F32)))
    idx = jnp.arange(block, dtype=F32)
    ch = jnp.arange(block) // chunk
    dist = jnp.abs(idx[:, None] - idx[None, :])
    decay = jnp.exp(log_gamma[:, None, None] * dist[None])
    mask = jnp.where((ch[None, :] <= ch[:, None])[None], decay, 0.0)
    qdec = jnp.exp((idx[:, None] + 1.0) * log_gamma[None, :])
    kdec = jnp.exp((block - 1.0 - idx)[:, None] * log_gamma[None, :])
    gs = jnp.exp(block * log_gamma)[None, :]
    return mask, qdec, kdec, gs


def _rope_tables(pos):
    half = RET_DK // 2
    inv = jnp.power(ROPE_BASE, -jnp.arange(half, dtype=F32) / half)
    ang = pos[:, None] * inv[None, :]
    return jnp.cos(ang), jnp.sin(ang)


def _sgu_body(s, t, emit_v, x_ref, mod_ref, g_ref, win_ref, lng_ref, lnb_ref,
              wmix_ref, bmix_ref, wout_ref, xo_ref, *maybe_v_ref):
    x = x_ref[...]
    mod = mod_ref[...]
    h = _modulated(x, g_ref[...], mod[:, 0:D_MODEL], mod[:, D_MODEL:2 * D_MODEL],
                   s, t).astype(BF16)
    groups = range(SGU_GROUPS)

    def cols(grp, base):
        return slice(base + grp * SGU_DG, base + (grp + 1) * SGU_DG)

    def proj(grp, base):
        return _dot(h, win_ref[:, cols(grp, base)])

    pv = [proj(0, SGU_D)]
    v = []
    for grp in groups:
        pv.append(proj(grp + 1, SGU_D) if grp + 1 < SGU_GROUPS else proj(0, 0))
        v.append(_gelu(pv[grp]))
    pu = [pv.pop(), proj(1, 0)]
    mu = sum(jnp.sum(vg, axis=-1, keepdims=True) for vg in v) * (1.0 / SGU_D)
    d = [vg - mu for vg in v]
    var = sum(jnp.sum(dg * dg, axis=-1, keepdims=True) for dg in d) * (1.0 / SGU_D)
    rs = lax.rsqrt(var + EPS)
    vb = []
    for grp in groups:
        vn = d[grp] * rs * lng_ref[:, cols(grp, 0)] + lnb_ref[:, cols(grp, 0)]
        if emit_v:
            maybe_v_ref[0][:, cols(grp, 0)] = vn
        vb.append(vn.astype(BF16))
    y = jnp.zeros((s * t, D_MODEL), F32)
    mixed = _dot(wmix_ref[0], vb[0])
    for grp in groups:
        if grp + 2 < SGU_GROUPS:
            pu.append(proj(grp + 2, 0))
        u = _gelu(pu[grp])
        cur = mixed
        if grp + 1 < SGU_GROUPS:
            mixed = _dot(wmix_ref[grp + 1], vb[grp + 1])
        gated = (u * (cur + bmix_ref[:, grp:grp + 1])).astype(BF16)
        y = y + _dot(gated, wout_ref[cols(grp, 0), :])
    xo_ref[...] = x + _rows(mod[:, 2 * D_MODEL:3 * D_MODEL], s, t) * y


def _sgu(x, mod, g, w_in, ln_g, ln_b, w_mix, b_mix, w_out, s, t, emit_v):
    nb, r, _ = x.shape
    tm = s * t
    out_specs = [pl.BlockSpec((None, tm, D_MODEL), lambda i, j: (i, j, 0))]
    out_shape = [jax.ShapeDtypeStruct(x.shape, F32)]
    if emit_v:
        out_specs.append(pl.BlockSpec((None, tm, SGU_D), lambda i, j: (i, j, 0)))
        out_shape.append(jax.ShapeDtypeStruct((nb, r, SGU_D), F32))
    return pl.pallas_call(
        functools.partial(_sgu_body, s, t, emit_v),
        grid=(nb, r // tm),
        in_specs=[
            pl.BlockSpec((None, tm, D_MODEL), lambda i, j: (i, j, 0)),
            pl.BlockSpec((None, s, N_MOD * D_MODEL), lambda i, j: (i, 0, 0)),
            _resident((1, D_MODEL)),
            _resident(w_in.shape),
            _resident((1, SGU_D)),
            _resident((1, SGU_D)),
            _resident(w_mix.shape),
            _resident(b_mix.shape),
            _resident(w_out.shape),
        ],
        out_specs=out_specs,
        out_shape=out_shape,
        compiler_params=_params(56),
        name="sgu_sample" if emit_v else "sgu_prompt",
    )(x, mod, g, w_in, ln_g, ln_b, w_mix, b_mix, w_out)


def _sgu_mix_tables(w_s, b_s, seq_len):
    length = min(seq_len, SGU_CHUNK)
    blk = jnp.arange(SGU_CHUNK) // RET_CHUNK
    w = jnp.where(blk[None, :] <= blk[:, None], w_s, 0)[:, :length, :length]
    reps = ROW_TILE // length
    tile = jnp.zeros((SGU_GROUPS, ROW_TILE, ROW_TILE), w.dtype)
    for r in range(reps):
        tile = tile.at[:, r * length:(r + 1) * length, r * length:(r + 1) * length].set(w)
    bias = jnp.tile(b_s[:, :length].T, (reps, 1))
    return tile.astype(BF16), bias


def _ffn_body(s, t, final_norm, x_ref, mod_ref, g_ref, wup_ref, cw_ref, cb_ref,
              wdn_ref, buf_ref, fg_ref, xo_ref, co_ref, carry_ref):
    tm = s * t
    x = x_ref[...]
    mod = mod_ref[...]
    h = _modulated(x, g_ref[...], mod[:, 3 * D_MODEL:4 * D_MODEL],
                   mod[:, 4 * D_MODEL:5 * D_MODEL], s, t).astype(BF16)
    if s == 1:
        @pl.when(pl.program_id(1) == 0)
        def _():
            carry_ref[...] = buf_ref[0]

    def up(c, w):
        return [_dot(h, wup_ref[:, base + c:base + c + w]) for base in (0, FFN_D)]

    def conv(a, lo, w):
        r1 = pltpu.roll(a, 1, 0)
        r2 = pltpu.roll(a, 2, 0)
        if s == 1:
            p0 = carry_ref[0:1, lo:lo + w]
            p1 = carry_ref[1:2, lo:lo + w]
            carry_ref[:, lo:lo + w] = a[tm - 2:, :]
            row = lax.broadcasted_iota(jnp.int32, (8, 1), 0)
            a1 = jnp.concatenate([jnp.where(row == 0, p1, r1[:8]), r1[8:]], axis=0)
            a2 = jnp.concatenate(
                [jnp.where(row == 0, p0, jnp.where(row == 1, p1, r2[:8])), r2[8:]], axis=0)
        else:
            p0 = _rows(buf_ref[:, 0, lo:lo + w], s, t)
            p1 = _rows(buf_ref[:, 1, lo:lo + w], s, t)
            co_ref[:, :, lo:lo + w] = a.reshape(s, t, w)[:, t - 2:, :]
            pos = lax.broadcasted_iota(jnp.int32, (tm, 1), 0) & (t - 1)
            a1 = jnp.where(pos == 0, p1, r1)
            a2 = jnp.where(pos == 0, p0, jnp.where(pos == 1, p1, r2))
        c = cb_ref[:, lo:lo + w] + a2 * cw_ref[0:1, lo:lo + w]
        c = c + a1 * cw_ref[1:2, lo:lo + w]
        return c + a * cw_ref[2:3, lo:lo + w]

    y = jnp.zeros((tm, D_MODEL), F32)
    nxt = up(*FFN_CHUNKS[0])
    for k, (c, w) in enumerate(FFN_CHUNKS):
        cur = nxt
        if k + 1 < len(FFN_CHUNKS):
            nxt = up(*FFN_CHUNKS[k + 1])
        gate = conv(cur[0], c, w)
        val = conv(cur[1], FFN_D + c, w)
        hid = (_silu(gate) * val).astype(BF16)
        y = y + _dot(hid, wdn_ref[c:c + w, :])
    if s == 1:
        co_ref[0] = carry_ref[...]
    xn = x + _rows(mod[:, 5 * D_MODEL:6 * D_MODEL], s, t) * y
    if final_norm:
        xn = _rmsnorm(xn, fg_ref[...])
    xo_ref[...] = xn


def _ffn(x, mod, g, w_up, conv_w, conv_b, w_down, buf, final_g, layer, s, t, final_norm, name):
    nb, r, _ = x.shape
    tm = s * t
    assert t & (t - 1) == 0 and t >= CONV_W - 1
    return pl.pallas_call(
        functools.partial(_ffn_body, s, t, final_norm),
        grid=(nb, r // tm),
        in_specs=[
            pl.BlockSpec((None, tm, D_MODEL), lambda i, j: (i, j, 0)),
            pl.BlockSpec((None, s, N_MOD * D_MODEL), lambda i, j: (i, 0, 0)),
            _resident((1, D_MODEL)),
            _resident(w_up.shape, layer),
            _resident((CONV_W, 2 * FFN_D)),
            _resident((1, 2 * FFN_D)),
            _resident(w_down.shape, layer),
            pl.BlockSpec((None, s, CONV_W - 1, 2 * FFN_D), lambda i, j: (i, 0, 0, 0)),
            _resident((1, D_MODEL)),
        ],
        out_specs=[
            pl.BlockSpec((None, tm, D_MODEL), lambda i, j: (i, j, 0)),
            pl.BlockSpec((None, s, CONV_W - 1, 2 * FFN_D), lambda i, j: (i, 0, 0, 0)),
        ],
        out_shape=[
            jax.ShapeDtypeStruct(x.shape, F32),
            jax.ShapeDtypeStruct(buf.shape, F32),
        ],
        scratch_shapes=[pltpu.VMEM((CONV_W - 1, 2 * FFN_D), F32)],
        compiler_params=_params(56),
        name=name,
    )(x, mod, g, w_up, conv_w, conv_b, w_down, buf, final_g)


def kernel(x_prompt, x_sample, state_ret, state_ffn_conv, c_prompt, c_sample,
           w_ada, b_ada, norm_mix_g, norm_ffn_g,
           ret_w_in, ret_gn_g, ret_w_out,
           sgu_w_in, sgu_ln_g, sgu_ln_b, sgu_w_s, sgu_b_s, sgu_w_out,
           ffn_w_up, ffn_conv_w, ffn_conv_b, ffn_w_down, final_g):
    bp, seq, _ = x_prompt.shape
    bs, dec, _ = x_sample.shape
    depth = w_ada.shape[0]
    ss = ROW_TILE // dec
    nbs = bs // ss

    mod = _ada(jnp.concatenate([c_prompt, c_sample], axis=0), w_ada, b_ada)
    mod_p = mod[:, :bp].reshape(depth, bp, 1, -1)
    mod_s = mod[:, bp:].reshape(depth, nbs, ss, -1)

    xp = x_prompt
    xs = x_sample.reshape(nbs, ROW_TILE, D_MODEL)
    fg = final_g.reshape(1, D_MODEL)
    zero_buf = jnp.zeros((bp, 1, CONV_W - 1, 2 * FFN_D), F32)
    w_up = ffn_w_up.astype(BF16)
    w_down = ffn_w_down.astype(BF16)

    ret_p, ret_s, conv_p, conv_s, sgu_s = [], [], [], [], []
    for i in range(depth):
        j = i // 2
        g_mix = norm_mix_g[i].reshape(1, D_MODEL)
        if i % 2 == 0:
            w_in = ret_w_in[j].astype(BF16)
            w_out = ret_w_out[j].astype(BF16)
            gn_g = ret_gn_g[j].reshape(1, RET_V)
            cos_p, sin_p = _rope_tables(jnp.arange(seq, dtype=F32))
            xp, sp = _ret_prompt(xp, mod_p[i], g_mix, w_in, cos_p, sin_p,
                                 *_decay_tables(ROW_TILE, RET_CHUNK), gn_g, w_out)
            cos_s, sin_s = _rope_tables(PAST_LEN + jnp.arange(dec, dtype=F32))
            proj = _proj_sample(xs, mod_s[i], g_mix, w_in, ss, dec)
            gated, s_new = _ret_sample_core(
                proj.reshape(bs, dec, -1), state_ret[j], cos_s, sin_s,
                *_decay_tables(dec, min(dec, RET_CHUNK)), gn_g)
            xs = _out_sample(gated.reshape(nbs, ROW_TILE, RET_V), xs, mod_s[i],
                             w_out, ss, dec)
            ret_p.append(sp)
            ret_s.append(s_new)
        else:
            w_in = sgu_w_in[j].astype(BF16)
            w_out = sgu_w_out[j].astype(BF16)
            ln_g = sgu_ln_g[j].reshape(1, SGU_D)
            ln_b = sgu_ln_b[j].reshape(1, SGU_D)
            wm_p, bm_p = _sgu_mix_tables(sgu_w_s[j], sgu_b_s[j], seq)
            wm_s, bm_s = _sgu_mix_tables(sgu_w_s[j], sgu_b_s[j], dec)
            (xp,) = _sgu(xp, mod_p[i], g_mix, w_in, ln_g, ln_b, wm_p, bm_p, w_out,
                         1, ROW_TILE, False)
            xs, vs = _sgu(xs, mod_s[i], g_mix, w_in, ln_g, ln_b, wm_s, bm_s, w_out,
                          ss, dec, True)
            sgu_s.append(vs.reshape(bs, dec, SGU_D))
        g_ffn = norm_ffn_g[i].reshape(1, D_MODEL)
        conv_b = ffn_conv_b[i].reshape(1, 2 * FFN_D)
        last = i == depth - 1
        xp, cp = _ffn(xp, mod_p[i], g_ffn, w_up, ffn_conv_w[i], conv_b, w_down,
                      zero_buf, fg, i, 1, FFN_ROW_TILE, last, "ffn_prompt_%d" % i)
        xs, cs = _ffn(xs, mod_s[i], g_ffn, w_up, ffn_conv_w[i], conv_b, w_down,
                      state_ffn_conv[i].reshape(nbs, ss, CONV_W - 1, 2 * FFN_D),
                      fg, i, ss, dec, last, "ffn_sample_%d" % i)
        conv_p.append(cp.reshape(bp, CONV_W - 1, 2 * FFN_D))
        conv_s.append(cs.reshape(bs, CONV_W - 1, 2 * FFN_D))

    return (xp, xs.reshape(bs, dec, D_MODEL), jnp.stack(ret_p), jnp.stack(ret_s),
            jnp.stack(conv_p), jnp.stack(conv_s), jnp.stack(sgu_s))
```

```python
import functools

import jax
import jax.numpy as jnp
from jax import lax
from jax.experimental import pallas as pl
from jax.experimental.pallas import tpu as pltpu

D_MODEL = 1024
N_MOD = 6
RET_HEADS = 4
RET_DK = 256
RET_DV = 512
RET_QK = RET_HEADS * RET_DK
RET_V = RET_HEADS * RET_DV
RET_CHUNK = 64
ROPE_BASE = 10000.0
PAST_LEN = 1024
SGU_CHUNK = 128
SGU_GROUPS = 4
SGU_D = 3 * D_MODEL
SGU_DG = SGU_D // SGU_GROUPS
FFN_D = 2816
CONV_W = 3
EPS = 1e-6

ROW_TILE = 256
FFN_ROW_TILE = 512
FFN_CHUNKS = ((0, 1024), (1024, 1024), (2048, 768))
MIB = 1024 * 1024
BF16_SUBLANES = 16

BF16 = jnp.bfloat16
F32 = jnp.float32


def _dot(a, b):
    return jnp.dot(a, b, preferred_element_type=F32)


def _resident(shape):
    nd = len(shape)
    return pl.BlockSpec(shape, lambda *_: (0,) * nd, pipeline_mode=pl.Buffered(1))


def _cast_specs(jobs, grid):
    steps = grid[0] * grid[1]
    in_specs, out_specs, out_shape = [], [], []
    for w, layer in jobs:
        _, rows, cols = w.shape
        parts = max(p for p in range(1, steps + 1)
                    if rows % p == 0 and (rows // p) % BF16_SUBLANES == 0)

        def slab(i, j, parts=parts):
            return jnp.minimum(i * grid[1] + j, parts - 1)

        in_specs.append(pl.BlockSpec((None, rows // parts, cols),
                                     lambda i, j, layer=layer, slab=slab: (layer, slab(i, j), 0)))
        out_specs.append(pl.BlockSpec((rows // parts, cols),
                                      lambda i, j, slab=slab: (slab(i, j), 0)))
        out_shape.append(jax.ShapeDtypeStruct((rows, cols), BF16))
    return in_specs, out_specs, out_shape


def _with_casts(body, n_in, n_out, n_cast):
    def wrapped(*refs):
        ins, refs = refs[:n_in], refs[n_in:]
        cast_in, refs = refs[:n_cast], refs[n_cast:]
        outs, refs = refs[:n_out], refs[n_out:]
        cast_out, scratch = refs[:n_cast], refs[n_cast:]
        for src, dst in zip(cast_in, cast_out):
            dst[...] = src[...].astype(BF16)
        body(*ins, *outs, *scratch)
    return wrapped


def _params(vmem_mib):
    return pltpu.CompilerParams(
        dimension_semantics=("arbitrary", "arbitrary"),
        vmem_limit_bytes=vmem_mib * MIB)


def _rows(v, s, t):
    if s == 1:
        return v
    return jnp.broadcast_to(v[:, None, :], (s, t, v.shape[-1])).reshape(s * t, v.shape[-1])


def _rmsnorm(x, g):
    ms = jnp.mean(x * x, axis=-1, keepdims=True)
    return x * lax.rsqrt(ms + EPS) * g


def _modulated(x, g, shift, scale, s, t):
    return _rmsnorm(x, g) * (1.0 + _rows(scale, s, t)) + _rows(shift, s, t)


def _silu(x):
    return x * jax.nn.sigmoid(x)


def _gelu(x):
    c = 2.0 * (2.0 / jnp.pi) ** 0.5
    return x * jax.nn.sigmoid(x * (c + (c * 0.044715) * (x * x)))


def _layernorm_nogain(o):
    mu = jnp.mean(o, axis=-1, keepdims=True)
    d = o - mu
    var = jnp.mean(d * d, axis=-1, keepdims=True)
    return d * lax.rsqrt(var + EPS)


def _ada_body(c_ref, w_ref, b_ref, o_ref):
    a = _silu(c_ref[...]).astype(BF16)
    o_ref[...] = _dot(a, w_ref[...].astype(BF16)) + b_ref[...]


def _ada(c_all, w_ada, b_ada):
    depth = w_ada.shape[0]
    n = c_all.shape[0]
    return pl.pallas_call(
        _ada_body,
        grid=(depth, N_MOD),
        in_specs=[
            pl.BlockSpec((n, D_MODEL), lambda i, j: (0, 0)),
            pl.BlockSpec((None, D_MODEL, D_MODEL), lambda i, j: (i, 0, j)),
            pl.BlockSpec((None, 1, D_MODEL), lambda i, j: (i, 0, j)),
        ],
        out_specs=pl.BlockSpec((None, n, D_MODEL), lambda i, j: (i, 0, j)),
        out_shape=jax.ShapeDtypeStruct((depth, n, N_MOD * D_MODEL), F32),
        compiler_params=_params(32),
        name="ada",
    )(c_all, w_ada, b_ada.reshape(depth, 1, N_MOD * D_MODEL))


def _rotary(x, cos, sin):
    half = x.shape[-1] // 2
    x1, x2 = x[:, :half], x[:, half:]
    return jnp.concatenate([x1 * cos - x2 * sin, x1 * sin + x2 * cos], axis=-1)


def _ret_head(q, k, v, s_prev, cos, sin, mask, qdec, kdec, gs):
    qb = (_rotary(q, cos, sin) * (RET_DK ** -0.5)).astype(BF16)
    kr = _rotary(k, cos, sin)
    kb = kr.astype(BF16)
    vb = v.astype(BF16)
    scores = lax.dot_general(qb, kb, (((1,), (1,)), ((), ())),
                             preferred_element_type=F32) * mask
    o = _dot(scores.astype(BF16), vb) + qdec * _dot(qb, s_prev.astype(BF16))
    kd = (kr * kdec).T.astype(BF16)
    s_new = gs * s_prev + _dot(kd, vb)
    return o, s_new


def _ret_prompt_body(x_ref, mod_ref, g_ref, win_ref, cos_ref, sin_ref, mask_ref,
                     qdec_ref, kdec_ref, gs_ref, gng_ref, wout_ref,
                     xo_ref, so_ref, s_ref):
    t = pl.program_id(1)

    @pl.when(t == 0)
    def _():
        s_ref[...] = jnp.zeros_like(s_ref)

    x = x_ref[...]
    mod = mod_ref[...]
    h = _modulated(x, g_ref[...], mod[:, 0:D_MODEL], mod[:, D_MODEL:2 * D_MODEL],
                   1, ROW_TILE).astype(BF16)
    cos = cos_ref[...]
    sin = sin_ref[...]
    y = jnp.zeros((ROW_TILE, D_MODEL), F32)
    for hd in range(RET_HEADS):
        q = _dot(h, win_ref[:, hd * RET_DK:(hd + 1) * RET_DK])
        k = _dot(h, win_ref[:, RET_QK + hd * RET_DK:RET_QK + (hd + 1) * RET_DK])
        v = _dot(h, win_ref[:, 2 * RET_QK + hd * RET_DV:2 * RET_QK + (hd + 1) * RET_DV])
        gate = _dot(h, win_ref[:, 2 * RET_QK + RET_V + hd * RET_DV:
                               2 * RET_QK + RET_V + (hd + 1) * RET_DV])
        o, s_new = _ret_head(q, k, v, s_ref[hd], cos, sin, mask_ref[hd],
                             qdec_ref[:, hd:hd + 1], kdec_ref[:, hd:hd + 1],
                             gs_ref[:, hd:hd + 1])
        s_ref[hd] = s_new
        on = _layernorm_nogain(o) * gng_ref[:, hd * RET_DV:(hd + 1) * RET_DV]
        gated = (_silu(gate) * on).astype(BF16)
        y = y + _dot(gated, wout_ref[hd * RET_DV:(hd + 1) * RET_DV, :])
    xo_ref[...] = x + mod[:, 2 * D_MODEL:3 * D_MODEL] * y

    @pl.when(t == pl.num_programs(1) - 1)
    def _():
        so_ref[...] = s_ref[...]


def _ret_prompt(x, mod, g, w_in, cos, sin, mask, qdec, kdec, gs, gn_g, w_out, casts=()):
    b, seq, _ = x.shape
    grid = (b, seq // ROW_TILE)
    in_specs = [
        pl.BlockSpec((None, ROW_TILE, D_MODEL), lambda i, t: (i, t, 0)),
        pl.BlockSpec((None, 1, N_MOD * D_MODEL), lambda i, t: (i, 0, 0)),
        _resident((1, D_MODEL)),
        _resident(w_in.shape),
        pl.BlockSpec((ROW_TILE, RET_DK // 2), lambda i, t: (t, 0)),
        pl.BlockSpec((ROW_TILE, RET_DK // 2), lambda i, t: (t, 0)),
        _resident(mask.shape),
        _resident(qdec.shape),
        _resident(kdec.shape),
        _resident(gs.shape),
        _resident((1, RET_V)),
        _resident(w_out.shape),
    ]
    out_specs = [
        pl.BlockSpec((None, ROW_TILE, D_MODEL), lambda i, t: (i, t, 0)),
        pl.BlockSpec((None, RET_HEADS, RET_DK, RET_DV), lambda i, t: (i, 0, 0, 0)),
    ]
    out_shape = [
        jax.ShapeDtypeStruct(x.shape, F32),
        jax.ShapeDtypeStruct((b, RET_HEADS, RET_DK, RET_DV), F32),
    ]
    cast_in, cast_out, cast_shape = _cast_specs(casts, grid)
    return pl.pallas_call(
        _with_casts(_ret_prompt_body, len(in_specs), len(out_specs), len(casts)),
        grid=grid,
        in_specs=in_specs + cast_in,
        out_specs=out_specs + cast_out,
        out_shape=out_shape + cast_shape,
        scratch_shapes=[pltpu.VMEM((RET_HEADS, RET_DK, RET_DV), F32)],
        compiler_params=_params(56),
        name="ret_prompt",
    )(x, mod.reshape(b, 1, -1), g, w_in, cos, sin, mask, qdec, kdec, gs, gn_g, w_out,
      *[w for w, _ in casts])


def _proj_body(s, t, x_ref, mod_ref, g_ref, w_ref, o_ref):
    mod = mod_ref[...]
    h = _modulated(x_ref[...], g_ref[...], mod[:, 0:D_MODEL],
                   mod[:, D_MODEL:2 * D_MODEL], s, t).astype(BF16)
    o_ref[...] = _dot(h, w_ref[...])


def _proj_sample(x, mod, g, w, s, t):
    nb, tm, _ = x.shape
    n = w.shape[1]
    return pl.pallas_call(
        functools.partial(_proj_body, s, t),
        grid=(nb, 1),
        in_specs=[
            pl.BlockSpec((None, tm, D_MODEL), lambda i, j: (i, 0, 0)),
            pl.BlockSpec((None, s, N_MOD * D_MODEL), lambda i, j: (i, 0, 0)),
            _resident((1, D_MODEL)),
            _resident(w.shape),
        ],
        out_specs=pl.BlockSpec((None, tm, n), lambda i, j: (i, 0, 0)),
        out_shape=jax.ShapeDtypeStruct((nb, tm, n), F32),
        compiler_params=_params(48),
        name="ret_sample_proj",
    )(x, mod, g, w)


def _ret_sample_core_body(p_ref, s0_ref, cos_ref, sin_ref, mask_ref, qdec_ref,
                          kdec_ref, gs_ref, gng_ref, o_ref, so_ref):
    cos = cos_ref[...]
    sin = sin_ref[...]
    for hd in range(RET_HEADS):
        q = p_ref[:, hd * RET_DK:(hd + 1) * RET_DK]
        k = p_ref[:, RET_QK + hd * RET_DK:RET_QK + (hd + 1) * RET_DK]
        v = p_ref[:, 2 * RET_QK + hd * RET_DV:2 * RET_QK + (hd + 1) * RET_DV]
        gate = p_ref[:, 2 * RET_QK + RET_V + hd * RET_DV:
                     2 * RET_QK + RET_V + (hd + 1) * RET_DV]
        o, s_new = _ret_head(q, k, v, s0_ref[hd], cos, sin, mask_ref[hd],
                             qdec_ref[:, hd:hd + 1], kdec_ref[:, hd:hd + 1],
                             gs_ref[:, hd:hd + 1])
        so_ref[hd] = s_new
        on = _layernorm_nogain(o) * gng_ref[:, hd * RET_DV:(hd + 1) * RET_DV]
        o_ref[:, hd * RET_DV:(hd + 1) * RET_DV] = _silu(gate) * on


def _ret_sample_core(proj, s0, cos, sin, mask, qdec, kdec, gs, gn_g):
    b, t, n = proj.shape
    return pl.pallas_call(
        _ret_sample_core_body,
        grid=(b, 1),
        in_specs=[
            pl.BlockSpec((None, t, n), lambda i, j: (i, 0, 0)),
            pl.BlockSpec((None, RET_HEADS, RET_DK, RET_DV), lambda i, j: (i, 0, 0, 0)),
            _resident(cos.shape),
            _resident(sin.shape),
            _resident(mask.shape),
            _resident(qdec.shape),
            _resident(kdec.shape),
            _resident(gs.shape),
            _resident((1, RET_V)),
        ],
        out_specs=[
            pl.BlockSpec((None, t, RET_V), lambda i, j: (i, 0, 0)),
            pl.BlockSpec((None, RET_HEADS, RET_DK, RET_DV), lambda i, j: (i, 0, 0, 0)),
        ],
        out_shape=[
            jax.ShapeDtypeStruct((b, t, RET_V), F32),
            jax.ShapeDtypeStruct(s0.shape, F32),
        ],
        compiler_params=_params(32),
        name="ret_sample_core",
    )(proj, s0, cos, sin, mask, qdec, kdec, gs, gn_g)


def _out_body(s, t, a_ref, x_ref, mod_ref, w_ref, o_ref):
    y = _dot(a_ref[...].astype(BF16), w_ref[...])
    gate = _rows(mod_ref[...][:, 2 * D_MODEL:3 * D_MODEL], s, t)
    o_ref[...] = x_ref[...] + gate * y


def _out_sample(a, x, mod, w, s, t):
    nb, tm, k = a.shape
    return pl.pallas_call(
        functools.partial(_out_body, s, t),
        grid=(nb, 1),
        in_specs=[
            pl.BlockSpec((None, tm, k), lambda i, j: (i, 0, 0)),
            pl.BlockSpec((None, tm, D_MODEL), lambda i, j: (i, 0, 0)),
            pl.BlockSpec((None, s, N_MOD * D_MODEL), lambda i, j: (i, 0, 0)),
            _resident(w.shape),
        ],
        out_specs=pl.BlockSpec((None, tm, D_MODEL), lambda i, j: (i, 0, 0)),
        out_shape=jax.ShapeDtypeStruct(x.shape, F32),
        compiler_params=_params(32),
        name="ret_sample_out",
    )(a, x, mod, w)


def _decay_tables(block, chunk):
    log_gamma = jnp.log(1.0 - jnp.exp2(-5.0 - jnp.arange(RET_HEADS, dtype=F32)))
    idx = jnp.arange(block, dtype=F32)
    ch = jnp.arange(block) // chunk
    dist = jnp.abs(idx[:, None] - idx[None, :])
    decay = jnp.exp(log_gamma[:, None, None] * dist[None])
    mask = jnp.where((ch[None, :] <= ch[:, None])[None], decay, 0.0)
    qdec = jnp.exp((idx[:, None] + 1.0) * log_gamma[None, :])
    kdec = jnp.exp((block - 1.0 - idx)[:, None] * log_gamma[None, :])
    gs = jnp.exp(block * log_gamma)[None, :]
    return mask, qdec, kdec, gs


def _rope_tables(pos):
    half = RET_DK // 2
    inv = jnp.power(ROPE_BASE, -jnp.arange(half, dtype=F32) / half)
    ang = pos[:, None] * inv[None, :]
    return jnp.cos(ang), jnp.sin(ang)


def _sgu_body(s, t, emit_v, x_ref, mod_ref, g_ref, win_ref, lng_ref, lnb_ref,
              wmix_ref, bmix_ref, wout_ref, xo_ref, *maybe_v_ref):
    x = x_ref[...]
    mod = mod_ref[...]
    h = _modulated(x, g_ref[...], mod[:, 0:D_MODEL], mod[:, D_MODEL:2 * D_MODEL],
                   s, t).astype(BF16)
    groups = range(SGU_GROUPS)

    def cols(grp, base):
        return slice(base + grp * SGU_DG, base + (grp + 1) * SGU_DG)

    def proj(grp, base):
        return _dot(h, win_ref[:, cols(grp, base)])

    pv = [proj(0, SGU_D)]
    v = []
    for grp in groups:
        pv.append(proj(grp + 1, SGU_D) if grp + 1 < SGU_GROUPS else proj(0, 0))
        v.append(_gelu(pv[grp]))
    pu = [pv.pop(), proj(1, 0)]
    mu = sum(jnp.sum(vg, axis=-1, keepdims=True) for vg in v) * (1.0 / SGU_D)
    d = [vg - mu for vg in v]
    var = sum(jnp.sum(dg * dg, axis=-1, keepdims=True) for dg in d) * (1.0 / SGU_D)
    rs = lax.rsqrt(var + EPS)
    vb = []
    for grp in groups:
        vn = d[grp] * rs * lng_ref[:, cols(grp, 0)] + lnb_ref[:, cols(grp, 0)]
        if emit_v:
            maybe_v_ref[0][:, cols(grp, 0)] = vn
        vb.append(vn.astype(BF16))
    y = jnp.zeros((s * t, D_MODEL), F32)
    mixed = _dot(wmix_ref[0], vb[0])
    for grp in groups:
        if grp + 2 < SGU_GROUPS:
            pu.append(proj(grp + 2, 0))
        u = _gelu(pu[grp])
        cur = mixed
        if grp + 1 < SGU_GROUPS:
            mixed = _dot(wmix_ref[grp + 1], vb[grp + 1])
        gated = (u * (cur + bmix_ref[:, grp:grp + 1])).astype(BF16)
        y = y + _dot(gated, wout_ref[cols(grp, 0), :])
    xo_ref[...] = x + _rows(mod[:, 2 * D_MODEL:3 * D_MODEL], s, t) * y


def _sgu(x, mod, g, w_in, ln_g, ln_b, w_mix, b_mix, w_out, s, t, emit_v, casts=()):
    nb, r, _ = x.shape
    tm = s * t
    grid = (nb, r // tm)
    in_specs = [
        pl.BlockSpec((None, tm, D_MODEL), lambda i, j: (i, j, 0)),
        pl.BlockSpec((None, s, N_MOD * D_MODEL), lambda i, j: (i, 0, 0)),
        _resident((1, D_MODEL)),
        _resident(w_in.shape),
        _resident((1, SGU_D)),
        _resident((1, SGU_D)),
        _resident(w_mix.shape),
        _resident(b_mix.shape),
        _resident(w_out.shape),
    ]
    out_specs = [pl.BlockSpec((None, tm, D_MODEL), lambda i, j: (i, j, 0))]
    out_shape = [jax.ShapeDtypeStruct(x.shape, F32)]
    if emit_v:
        out_specs.append(pl.BlockSpec((None, tm, SGU_D), lambda i, j: (i, j, 0)))
        out_shape.append(jax.ShapeDtypeStruct((nb, r, SGU_D), F32))
    cast_in, cast_out, cast_shape = _cast_specs(casts, grid)
    return pl.pallas_call(
        _with_casts(functools.partial(_sgu_body, s, t, emit_v),
                    len(in_specs), len(out_specs), len(casts)),
        grid=grid,
        in_specs=in_specs + cast_in,
        out_specs=out_specs + cast_out,
        out_shape=out_shape + cast_shape,
        compiler_params=_params(56),
        name="sgu_sample" if emit_v else "sgu_prompt",
    )(x, mod, g, w_in, ln_g, ln_b, w_mix, b_mix, w_out, *[w for w, _ in casts])


def _sgu_mix_tables(w_s, b_s, seq_len):
    length = min(seq_len, SGU_CHUNK)
    blk = jnp.arange(SGU_CHUNK) // RET_CHUNK
    w = jnp.where(blk[None, :] <= blk[:, None], w_s, 0)[:, :length, :length]
    reps = ROW_TILE // length
    tile = jnp.zeros((SGU_GROUPS, ROW_TILE, ROW_TILE), w.dtype)
    for r in range(reps):
        tile = tile.at[:, r * length:(r + 1) * length, r * length:(r + 1) * length].set(w)
    bias = jnp.tile(b_s[:, :length].T, (reps, 1))
    return tile.astype(BF16), bias


def _ffn_body(s, t, final_norm, x_ref, mod_ref, g_ref, wup_ref, cw_ref, cb_ref,
              wdn_ref, buf_ref, fg_ref, xo_ref, co_ref, carry_ref):
    tm = s * t
    x = x_ref[...]
    mod = mod_ref[...]
    h = _modulated(x, g_ref[...], mod[:, 3 * D_MODEL:4 * D_MODEL],
                   mod[:, 4 * D_MODEL:5 * D_MODEL], s, t).astype(BF16)
    if s == 1:
        @pl.when(pl.program_id(1) == 0)
        def _():
            carry_ref[...] = buf_ref[0]

    def up(c, w):
        return [_dot(h, wup_ref[:, base + c:base + c + w]) for base in (0, FFN_D)]

    def conv(a, lo, w):
        r1 = pltpu.roll(a, 1, 0)
        r2 = pltpu.roll(a, 2, 0)
        if s == 1:
            p0 = carry_ref[0:1, lo:lo + w]
            p1 = carry_ref[1:2, lo:lo + w]
            carry_ref[:, lo:lo + w] = a[tm - 2:, :]
            row = lax.broadcasted_iota(jnp.int32, (8, 1), 0)
            a1 = jnp.concatenate([jnp.where(row == 0, p1, r1[:8]), r1[8:]], axis=0)
            a2 = jnp.concatenate(
                [jnp.where(row == 0, p0, jnp.where(row == 1, p1, r2[:8])), r2[8:]], axis=0)
        else:
            p0 = _rows(buf_ref[:, 0, lo:lo + w], s, t)
            p1 = _rows(buf_ref[:, 1, lo:lo + w], s, t)
            co_ref[:, :, lo:lo + w] = a.reshape(s, t, w)[:, t - 2:, :]
            pos = lax.broadcasted_iota(jnp.int32, (tm, 1), 0) & (t - 1)
            a1 = jnp.where(pos == 0, p1, r1)
            a2 = jnp.where(pos == 0, p0, jnp.where(pos == 1, p1, r2))
        c = cb_ref[:, lo:lo + w] + a2 * cw_ref[0:1, lo:lo + w]
        c = c + a1 * cw_ref[1:2, lo:lo + w]
        return c + a * cw_ref[2:3, lo:lo + w]

    y = jnp.zeros((tm, D_MODEL), F32)
    nxt = up(*FFN_CHUNKS[0])
    for k, (c, w) in enumerate(FFN_CHUNKS):
        cur = nxt
        if k + 1 < len(FFN_CHUNKS):
            nxt = up(*FFN_CHUNKS[k + 1])
        gate = conv(cur[0], c, w)
        val = conv(cur[1], FFN_D + c, w)
        hid = (_silu(gate) * val).astype(BF16)
        y = y + _dot(hid, wdn_ref[c:c + w, :])
    if s == 1:
        co_ref[0] = carry_ref[...]
    xn = x + _rows(mod[:, 5 * D_MODEL:6 * D_MODEL], s, t) * y
    if final_norm:
        xn = _rmsnorm(xn, fg_ref[...])
    xo_ref[...] = xn


def _ffn(x, mod, g, w_up, conv_w, conv_b, w_down, buf, final_g, s, t, final_norm, name,
         casts=()):
    nb, r, _ = x.shape
    tm = s * t
    assert t & (t - 1) == 0 and t >= CONV_W - 1
    grid = (nb, r // tm)
    in_specs = [
        pl.BlockSpec((None, tm, D_MODEL), lambda i, j: (i, j, 0)),
        pl.BlockSpec((None, s, N_MOD * D_MODEL), lambda i, j: (i, 0, 0)),
        _resident((1, D_MODEL)),
        _resident(w_up.shape),
        _resident((CONV_W, 2 * FFN_D)),
        _resident((1, 2 * FFN_D)),
        _resident(w_down.shape),
        pl.BlockSpec((None, s, CONV_W - 1, 2 * FFN_D), lambda i, j: (i, 0, 0, 0)),
        _resident((1, D_MODEL)),
    ]
    out_specs = [
        pl.BlockSpec((None, tm, D_MODEL), lambda i, j: (i, j, 0)),
        pl.BlockSpec((None, s, CONV_W - 1, 2 * FFN_D), lambda i, j: (i, 0, 0, 0)),
    ]
    out_shape = [
        jax.ShapeDtypeStruct(x.shape, F32),
        jax.ShapeDtypeStruct(buf.shape, F32),
    ]
    cast_in, cast_out, cast_shape = _cast_specs(casts, grid)
    return pl.pallas_call(
        _with_casts(functools.partial(_ffn_body, s, t, final_norm),
                    len(in_specs), len(out_specs), len(casts)),
        grid=grid,
        in_specs=in_specs + cast_in,
        out_specs=out_specs + cast_out,
        out_shape=out_shape + cast_shape,
        scratch_shapes=[pltpu.VMEM((CONV_W - 1, 2 * FFN_D), F32)],
        compiler_params=_params(56),
        name=name,
    )(x, mod, g, w_up, conv_w, conv_b, w_down, buf, final_g, *[w for w, _ in casts])


def kernel(x_prompt, x_sample, state_ret, state_ffn_conv, c_prompt, c_sample,
           w_ada, b_ada, norm_mix_g, norm_ffn_g,
           ret_w_in, ret_gn_g, ret_w_out,
           sgu_w_in, sgu_ln_g, sgu_ln_b, sgu_w_s, sgu_b_s, sgu_w_out,
           ffn_w_up, ffn_conv_w, ffn_conv_b, ffn_w_down, final_g):
    bp, seq, _ = x_prompt.shape
    bs, dec, _ = x_sample.shape
    depth = w_ada.shape[0]
    ss = ROW_TILE // dec
    nbs = bs // ss

    mod = _ada(jnp.concatenate([c_prompt, c_sample], axis=0), w_ada, b_ada)
    mod_p = mod[:, :bp].reshape(depth, bp, 1, -1)
    mod_s = mod[:, bp:].reshape(depth, nbs, ss, -1)

    xp = x_prompt
    xs = x_sample.reshape(nbs, ROW_TILE, D_MODEL)
    fg = final_g.reshape(1, D_MODEL)
    zero_buf = jnp.zeros((bp, 1, CONV_W - 1, 2 * FFN_D), F32)

    def mixer_weights(layer):
        src = (ret_w_in, ret_w_out) if layer % 2 == 0 else (sgu_w_in, sgu_w_out)
        return [(w, layer // 2) for w in src]

    w_in, w_out = [w[j].astype(BF16) for w, j in mixer_weights(0)]

    ret_p, ret_s, conv_p, conv_s, sgu_s = [], [], [], [], []
    for i in range(depth):
        j = i // 2
        g_mix = norm_mix_g[i].reshape(1, D_MODEL)
        ffn_casts = [(ffn_w_up, i), (ffn_w_down, i)]
        if i % 2 == 0:
            gn_g = ret_gn_g[j].reshape(1, RET_V)
            cos_p, sin_p = _rope_tables(jnp.arange(seq, dtype=F32))
            xp, sp, w_up, w_down = _ret_prompt(
                xp, mod_p[i], g_mix, w_in, cos_p, sin_p,
                *_decay_tables(ROW_TILE, RET_CHUNK), gn_g, w_out, casts=ffn_casts)
            cos_s, sin_s = _rope_tables(PAST_LEN + jnp.arange(dec, dtype=F32))
            proj = _proj_sample(xs, mod_s[i], g_mix, w_in, ss, dec)
            gated, s_new = _ret_sample_core(
                proj.reshape(bs, dec, -1), state_ret[j], cos_s, sin_s,
                *_decay_tables(dec, min(dec, RET_CHUNK)), gn_g)
            xs = _out_sample(gated.reshape(nbs, ROW_TILE, RET_V), xs, mod_s[i],
                             w_out, ss, dec)
            ret_p.append(sp)
            ret_s.append(s_new)
        else:
            ln_g = sgu_ln_g[j].reshape(1, SGU_D)
            ln_b = sgu_ln_b[j].reshape(1, SGU_D)
            wm_p, bm_p = _sgu_mix_tables(sgu_w_s[j], sgu_b_s[j], seq)
            wm_s, bm_s = _sgu_mix_tables(sgu_w_s[j], sgu_b_s[j], dec)
            xp, w_up, w_down = _sgu(xp, mod_p[i], g_mix, w_in, ln_g, ln_b, wm_p, bm_p,
                                    w_out, 1, ROW_TILE, False, casts=ffn_casts)
            xs, vs = _sgu(xs, mod_s[i], g_mix, w_in, ln_g, ln_b, wm_s, bm_s, w_out,
                          ss, dec, True)
            sgu_s.append(vs.reshape(bs, dec, SGU_D))
        g_ffn = norm_ffn_g[i].reshape(1, D_MODEL)
        conv_b = ffn_conv_b[i].reshape(1, 2 * FFN_D)
        last = i == depth - 1
        xp, cp, *next_mixer = _ffn(
            xp, mod_p[i], g_ffn, w_up, ffn_conv_w[i], conv_b, w_down, zero_buf, fg,
            1, FFN_ROW_TILE, last, "ffn_prompt_%d" % i,
            casts=[] if last else mixer_weights(i + 1))
        xs, cs = _ffn(xs, mod_s[i], g_ffn, w_up, ffn_conv_w[i], conv_b, w_down,
                      state_ffn_conv[i].reshape(nbs, ss, CONV_W - 1, 2 * FFN_D),
                      fg, ss, dec, last, "ffn_sample_%d" % i)
        if not last:
            w_in, w_out = next_mixer
        conv_p.append(cp.reshape(bp, CONV_W - 1, 2 * FFN_D))
        conv_s.append(cs.reshape(bs, CONV_W - 1, 2 * FFN_D))

    return (xp, xs.reshape(bs, dec, D_MODEL), jnp.stack(ret_p), jnp.stack(ret_s),
            jnp.stack(conv_p), jnp.stack(conv_s), jnp.stack(sgu_s))
```

```python
import functools

import jax
import jax.numpy as jnp
import numpy as np
from jax import lax
from jax.experimental import pallas as pl
from jax.experimental.pallas import tpu as pltpu

D_MODEL = 1024
N_MOD = 6
RET_HEADS = 4
RET_DK = 256
RET_DV = 512
RET_QK = RET_HEADS * RET_DK
RET_V = RET_HEADS * RET_DV
RET_CHUNK = 64
ROPE_BASE = 10000.0
PAST_LEN = 1024
SGU_CHUNK = 128
SGU_GROUPS = 4
SGU_D = 3 * D_MODEL
SGU_DG = SGU_D // SGU_GROUPS
FFN_D = 2816
CONV_W = 3
EPS = 1e-6

ROW_TILE = 256
FFN_ROW_TILE = 512
RET_SAMPLE_STREAMS = 4
FFN_CHUNKS = ((0, 1024), (1024, 1024), (2048, 768))
MIB = 1024 * 1024
BF16_SUBLANES = 16

BF16 = jnp.bfloat16
F32 = jnp.float32


def _dot(a, b):
    return jnp.dot(a, b, preferred_element_type=F32)


def _resident(shape):
    nd = len(shape)
    return pl.BlockSpec(shape, lambda *_: (0,) * nd, pipeline_mode=pl.Buffered(1))


def _cast_specs(jobs, grid):
    steps = grid[0] * grid[1]
    in_specs, out_specs, out_shape = [], [], []
    for w, layer in jobs:
        _, rows, cols = w.shape
        parts = max(p for p in range(1, steps + 1)
                    if rows % p == 0 and (rows // p) % BF16_SUBLANES == 0)

        def slab(i, j, parts=parts):
            return jnp.minimum(i * grid[1] + j, parts - 1)

        in_specs.append(pl.BlockSpec((None, rows // parts, cols),
                                     lambda i, j, layer=layer, slab=slab: (layer, slab(i, j), 0)))
        out_specs.append(pl.BlockSpec((rows // parts, cols),
                                      lambda i, j, slab=slab: (slab(i, j), 0)))
        out_shape.append(jax.ShapeDtypeStruct((rows, cols), BF16))
    return in_specs, out_specs, out_shape


def _with_casts(body, n_in, n_out, n_cast):
    def wrapped(*refs):
        ins, refs = refs[:n_in], refs[n_in:]
        cast_in, refs = refs[:n_cast], refs[n_cast:]
        outs, refs = refs[:n_out], refs[n_out:]
        cast_out, scratch = refs[:n_cast], refs[n_cast:]
        for src, dst in zip(cast_in, cast_out):
            dst[...] = src[...].astype(BF16)
        body(*ins, *outs, *scratch)
    return wrapped


def _params(vmem_mib):
    return pltpu.CompilerParams(
        dimension_semantics=("arbitrary", "arbitrary"),
        vmem_limit_bytes=vmem_mib * MIB)


def _rows(v, s, t):
    if s == 1:
        return v
    return jnp.broadcast_to(v[:, None, :], (s, t, v.shape[-1])).reshape(s * t, v.shape[-1])


def _rmsnorm(x, g):
    ms = jnp.mean(x * x, axis=-1, keepdims=True)
    return x * lax.rsqrt(ms + EPS) * g


def _modulated(x, g, shift, scale, s, t):
    return _rmsnorm(x, g) * (1.0 + _rows(scale, s, t)) + _rows(shift, s, t)


def _silu(x):
    return x * jax.nn.sigmoid(x)


def _gelu(x):
    c = 2.0 * (2.0 / jnp.pi) ** 0.5
    return x * jax.nn.sigmoid(x * (c + (c * 0.044715) * (x * x)))


def _layernorm_nogain(o):
    mu = jnp.mean(o, axis=-1, keepdims=True)
    d = o - mu
    var = jnp.mean(d * d, axis=-1, keepdims=True)
    return d * lax.rsqrt(var + EPS)


def _ada_body(c_ref, w_ref, b_ref, o_ref):
    a = _silu(c_ref[...]).astype(BF16)
    o_ref[...] = _dot(a, w_ref[...].astype(BF16)) + b_ref[...]


def _ada(c_all, w_ada, b_ada):
    depth = w_ada.shape[0]
    n = c_all.shape[0]
    return pl.pallas_call(
        _ada_body,
        grid=(depth, N_MOD),
        in_specs=[
            pl.BlockSpec((n, D_MODEL), lambda i, j: (0, 0)),
            pl.BlockSpec((None, D_MODEL, D_MODEL), lambda i, j: (i, 0, j)),
            pl.BlockSpec((None, 1, D_MODEL), lambda i, j: (i, 0, j)),
        ],
        out_specs=pl.BlockSpec((None, n, D_MODEL), lambda i, j: (i, 0, j)),
        out_shape=jax.ShapeDtypeStruct((depth, n, N_MOD * D_MODEL), F32),
        compiler_params=_params(32),
        name="ada",
    )(c_all, w_ada, b_ada.reshape(depth, 1, N_MOD * D_MODEL))


def _rotary(x, cos, sin):
    half = x.shape[-1] // 2
    x1, x2 = x[:, :half], x[:, half:]
    return jnp.concatenate([x1 * cos - x2 * sin, x1 * sin + x2 * cos], axis=-1)


def _ret_head(q, k, v, s_prev, cos, sin, mask, qdec, kdec, gs):
    qb = (_rotary(q, cos, sin) * (RET_DK ** -0.5)).astype(BF16)
    kr = _rotary(k, cos, sin)
    kb = kr.astype(BF16)
    vb = v.astype(BF16)
    scores = lax.dot_general(qb, kb, (((1,), (1,)), ((), ())),
                             preferred_element_type=F32) * mask
    o = _dot(scores.astype(BF16), vb) + qdec * _dot(qb, s_prev.astype(BF16))
    kd = (kr * kdec).T.astype(BF16)
    s_new = gs * s_prev + _dot(kd, vb)
    return o, s_new


def _ret_prompt_body(x_ref, mod_ref, g_ref, win_ref, cos_ref, sin_ref, mask_ref,
                     qdec_ref, kdec_ref, gs_ref, gng_ref, wout_ref,
                     xo_ref, so_ref, s_ref):
    t = pl.program_id(1)

    @pl.when(t == 0)
    def _():
        s_ref[...] = jnp.zeros_like(s_ref)

    x = x_ref[...]
    mod = mod_ref[...]
    h = _modulated(x, g_ref[...], mod[:, 0:D_MODEL], mod[:, D_MODEL:2 * D_MODEL],
                   1, ROW_TILE).astype(BF16)
    cos = cos_ref[...]
    sin = sin_ref[...]
    y = jnp.zeros((ROW_TILE, D_MODEL), F32)
    for hd in range(RET_HEADS):
        q = _dot(h, win_ref[:, hd * RET_DK:(hd + 1) * RET_DK])
        k = _dot(h, win_ref[:, RET_QK + hd * RET_DK:RET_QK + (hd + 1) * RET_DK])
        v = _dot(h, win_ref[:, 2 * RET_QK + hd * RET_DV:2 * RET_QK + (hd + 1) * RET_DV])
        gate = _dot(h, win_ref[:, 2 * RET_QK + RET_V + hd * RET_DV:
                               2 * RET_QK + RET_V + (hd + 1) * RET_DV])
        o, s_new = _ret_head(q, k, v, s_ref[hd], cos, sin, mask_ref[hd],
                             qdec_ref[:, hd:hd + 1], kdec_ref[:, hd:hd + 1],
                             gs_ref[:, hd:hd + 1])
        s_ref[hd] = s_new
        on = _layernorm_nogain(o) * gng_ref[:, hd * RET_DV:(hd + 1) * RET_DV]
        gated = (_silu(gate) * on).astype(BF16)
        y = y + _dot(gated, wout_ref[hd * RET_DV:(hd + 1) * RET_DV, :])
    xo_ref[...] = x + mod[:, 2 * D_MODEL:3 * D_MODEL] * y

    @pl.when(t == pl.num_programs(1) - 1)
    def _():
        so_ref[...] = s_ref[...]


def _ret_prompt(x, mod, g, w_in, cos, sin, mask, qdec, kdec, gs, gn_g, w_out, casts=()):
    b, seq, _ = x.shape
    grid = (b, seq // ROW_TILE)
    in_specs = [
        pl.BlockSpec((None, ROW_TILE, D_MODEL), lambda i, t: (i, t, 0)),
        pl.BlockSpec((None, 1, N_MOD * D_MODEL), lambda i, t: (i, 0, 0)),
        _resident((1, D_MODEL)),
        _resident(w_in.shape),
        pl.BlockSpec((ROW_TILE, RET_DK // 2), lambda i, t: (t, 0)),
        pl.BlockSpec((ROW_TILE, RET_DK // 2), lambda i, t: (t, 0)),
        _resident(mask.shape),
        _resident(qdec.shape),
        _resident(kdec.shape),
        _resident(gs.shape),
        _resident((1, RET_V)),
        _resident(w_out.shape),
    ]
    out_specs = [
        pl.BlockSpec((None, ROW_TILE, D_MODEL), lambda i, t: (i, t, 0)),
        pl.BlockSpec((None, RET_HEADS, RET_DK, RET_DV), lambda i, t: (i, 0, 0, 0)),
    ]
    out_shape = [
        jax.ShapeDtypeStruct(x.shape, F32),
        jax.ShapeDtypeStruct((b, RET_HEADS, RET_DK, RET_DV), F32),
    ]
    cast_in, cast_out, cast_shape = _cast_specs(casts, grid)
    return pl.pallas_call(
        _with_casts(_ret_prompt_body, len(in_specs), len(out_specs), len(casts)),
        grid=grid,
        in_specs=in_specs + cast_in,
        out_specs=out_specs + cast_out,
        out_shape=out_shape + cast_shape,
        scratch_shapes=[pltpu.VMEM((RET_HEADS, RET_DK, RET_DV), F32)],
        compiler_params=_params(56),
        name="ret_prompt",
    )(x, mod.reshape(b, 1, -1), g, w_in, cos, sin, mask, qdec, kdec, gs, gn_g, w_out,
      *[w for w, _ in casts])


def _proj_body(s, t, x_ref, mod_ref, g_ref, w_ref, o_ref):
    mod = mod_ref[...]
    h = _modulated(x_ref[...], g_ref[...], mod[:, 0:D_MODEL],
                   mod[:, D_MODEL:2 * D_MODEL], s, t).astype(BF16)
    o_ref[...] = _dot(h, w_ref[...])


def _proj_sample(x, mod, g, w, s, t):
    nb, tm, _ = x.shape
    n = w.shape[1]
    return pl.pallas_call(
        functools.partial(_proj_body, s, t),
        grid=(nb, 1),
        in_specs=[
            pl.BlockSpec((None, tm, D_MODEL), lambda i, j: (i, 0, 0)),
            pl.BlockSpec((None, s, N_MOD * D_MODEL), lambda i, j: (i, 0, 0)),
            _resident((1, D_MODEL)),
            _resident(w.shape),
        ],
        out_specs=pl.BlockSpec((None, tm, n), lambda i, j: (i, 0, 0)),
        out_shape=jax.ShapeDtypeStruct((nb, tm, n), F32),
        compiler_params=_params(48),
        name="ret_sample_proj",
    )(x, mod, g, w)


def _ret_sample_core_body(p_ref, s0_ref, cos_ref, sin_ref, mask_ref, qdec_ref,
                          kdec_ref, gs_ref, gng_ref, o_ref, so_ref):
    cos = cos_ref[...]
    sin = sin_ref[...]
    for b in range(RET_SAMPLE_STREAMS):
        for hd in range(RET_HEADS):
            q = p_ref[b, :, hd * RET_DK:(hd + 1) * RET_DK]
            k = p_ref[b, :, RET_QK + hd * RET_DK:RET_QK + (hd + 1) * RET_DK]
            v = p_ref[b, :, 2 * RET_QK + hd * RET_DV:2 * RET_QK + (hd + 1) * RET_DV]
            gate = p_ref[b, :, 2 * RET_QK + RET_V + hd * RET_DV:
                         2 * RET_QK + RET_V + (hd + 1) * RET_DV]
            o, s_new = _ret_head(q, k, v, s0_ref[b, hd], cos, sin, mask_ref[hd],
                                 qdec_ref[:, hd:hd + 1], kdec_ref[:, hd:hd + 1],
                                 gs_ref[:, hd:hd + 1])
            so_ref[b, hd] = s_new
            on = _layernorm_nogain(o) * gng_ref[:, hd * RET_DV:(hd + 1) * RET_DV]
            o_ref[b, :, hd * RET_DV:(hd + 1) * RET_DV] = _silu(gate) * on


def _ret_sample_core(proj, s0, cos, sin, mask, qdec, kdec, gs, gn_g):
    b, t, n = proj.shape
    nb = RET_SAMPLE_STREAMS
    return pl.pallas_call(
        _ret_sample_core_body,
        grid=(b // nb, 1),
        in_specs=[
            pl.BlockSpec((nb, t, n), lambda i, j: (i, 0, 0)),
            pl.BlockSpec((nb, RET_HEADS, RET_DK, RET_DV), lambda i, j: (i, 0, 0, 0)),
            _resident(cos.shape),
            _resident(sin.shape),
            _resident(mask.shape),
            _resident(qdec.shape),
            _resident(kdec.shape),
            _resident(gs.shape),
            _resident((1, RET_V)),
        ],
        out_specs=[
            pl.BlockSpec((nb, t, RET_V), lambda i, j: (i, 0, 0)),
            pl.BlockSpec((nb, RET_HEADS, RET_DK, RET_DV), lambda i, j: (i, 0, 0, 0)),
        ],
        out_shape=[
            jax.ShapeDtypeStruct((b, t, RET_V), F32),
            jax.ShapeDtypeStruct(s0.shape, F32),
        ],
        compiler_params=_params(48),
        name="ret_sample_core",
    )(proj, s0, cos, sin, mask, qdec, kdec, gs, gn_g)


def _out_body(s, t, a_ref, x_ref, mod_ref, w_ref, o_ref):
    y = _dot(a_ref[...].astype(BF16), w_ref[...])
    gate = _rows(mod_ref[...][:, 2 * D_MODEL:3 * D_MODEL], s, t)
    o_ref[...] = x_ref[...] + gate * y


def _out_sample(a, x, mod, w, s, t):
    nb, tm, k = a.shape
    return pl.pallas_call(
        functools.partial(_out_body, s, t),
        grid=(nb, 1),
        in_specs=[
            pl.BlockSpec((None, tm, k), lambda i, j: (i, 0, 0)),
            pl.BlockSpec((None, tm, D_MODEL), lambda i, j: (i, 0, 0)),
            pl.BlockSpec((None, s, N_MOD * D_MODEL), lambda i, j: (i, 0, 0)),
            _resident(w.shape),
        ],
        out_specs=pl.BlockSpec((None, tm, D_MODEL), lambda i, j: (i, 0, 0)),
        out_shape=jax.ShapeDtypeStruct(x.shape, F32),
        compiler_params=_params(32),
        name="ret_sample_out",
    )(a, x, mod, w)


def _decay_tables(block, chunk):
    log_gamma = np.log(1.0 - np.exp2(-5.0 - np.arange(RET_HEADS)))
    idx = np.arange(block, dtype=np.float64)
    ch = np.arange(block) // chunk
    dist = np.abs(idx[:, None] - idx[None, :])
    decay = np.exp(log_gamma[:, None, None] * dist[None])
    mask = np.where((ch[None, :] <= ch[:, None])[None], decay, 0.0)
    qdec = np.exp((idx[:, None] + 1.0) * log_gamma[None, :])
    kdec = np.exp((block - 1.0 - idx)[:, None] * log_gamma[None, :])
    gs = np.exp(block * log_gamma)[None, :]
    return tuple(jnp.asarray(a, F32) for a in (mask, qdec, kdec, gs))


def _rope_tables(first, count):
    half = RET_DK // 2
    inv = np.power(ROPE_BASE, -np.arange(half, dtype=np.float64) / half)
    ang = (first + np.arange(count, dtype=np.float64))[:, None] * inv[None, :]
    return jnp.asarray(np.cos(ang), F32), jnp.asarray(np.sin(ang), F32)


def _sgu_body(s, t, emit_v, x_ref, mod_ref, g_ref, win_ref, lng_ref, lnb_ref,
              wmix_ref, bmix_ref, wout_ref, xo_ref, *maybe_v_ref):
    x = x_ref[...]
    mod = mod_ref[...]
    h = _modulated(x, g_ref[...], mod[:, 0:D_MODEL], mod[:, D_MODEL:2 * D_MODEL],
                   s, t).astype(BF16)
    groups = range(SGU_GROUPS)

    def cols(grp, base):
        return slice(base + grp * SGU_DG, base + (grp + 1) * SGU_DG)

    def proj(grp, base):
        return _dot(h, win_ref[:, cols(grp, base)])

    pv = [proj(0, SGU_D)]
    v = []
    for grp in groups:
        pv.append(proj(grp + 1, SGU_D) if grp + 1 < SGU_GROUPS else proj(0, 0))
        v.append(_gelu(pv[grp]))
    pu = [pv.pop(), proj(1, 0)]
    mu = sum(jnp.sum(vg, axis=-1, keepdims=True) for vg in v) * (1.0 / SGU_D)
    d = [vg - mu for vg in v]
    var = sum(jnp.sum(dg * dg, axis=-1, keepdims=True) for dg in d) * (1.0 / SGU_D)
    rs = lax.rsqrt(var + EPS)
    vb = []
    for grp in groups:
        vn = d[grp] * rs * lng_ref[:, cols(grp, 0)] + lnb_ref[:, cols(grp, 0)]
        if emit_v:
            maybe_v_ref[0][:, cols(grp, 0)] = vn
        vb.append(vn.astype(BF16))
    y = jnp.zeros((s * t, D_MODEL), F32)
    mixed = _dot(wmix_ref[0], vb[0])
    for grp in groups:
        if grp + 2 < SGU_GROUPS:
            pu.append(proj(grp + 2, 0))
        u = _gelu(pu[grp])
        cur = mixed
        if grp + 1 < SGU_GROUPS:
            mixed = _dot(wmix_ref[grp + 1], vb[grp + 1])
        gated = (u * (cur + bmix_ref[:, grp:grp + 1])).astype(BF16)
        y = y + _dot(gated, wout_ref[cols(grp, 0), :])
    xo_ref[...] = x + _rows(mod[:, 2 * D_MODEL:3 * D_MODEL], s, t) * y


def _sgu(x, mod, g, w_in, ln_g, ln_b, w_mix, b_mix, w_out, s, t, emit_v, casts=()):
    nb, r, _ = x.shape
    tm = s * t
    grid = (nb, r // tm)
    in_specs = [
        pl.BlockSpec((None, tm, D_MODEL), lambda i, j: (i, j, 0)),
        pl.BlockSpec((None, s, N_MOD * D_MODEL), lambda i, j: (i, 0, 0)),
        _resident((1, D_MODEL)),
        _resident(w_in.shape),
        _resident((1, SGU_D)),
        _resident((1, SGU_D)),
        _resident(w_mix.shape),
        _resident(b_mix.shape),
        _resident(w_out.shape),
    ]
    out_specs = [pl.BlockSpec((None, tm, D_MODEL), lambda i, j: (i, j, 0))]
    out_shape = [jax.ShapeDtypeStruct(x.shape, F32)]
    if emit_v:
        out_specs.append(pl.BlockSpec((None, tm, SGU_D), lambda i, j: (i, j, 0)))
        out_shape.append(jax.ShapeDtypeStruct((nb, r, SGU_D), F32))
    cast_in, cast_out, cast_shape = _cast_specs(casts, grid)
    return pl.pallas_call(
        _with_casts(functools.partial(_sgu_body, s, t, emit_v),
                    len(in_specs), len(out_specs), len(casts)),
        grid=grid,
        in_specs=in_specs + cast_in,
        out_specs=out_specs + cast_out,
        out_shape=out_shape + cast_shape,
        compiler_params=_params(56),
        name="sgu_sample" if emit_v else "sgu_prompt",
    )(x, mod, g, w_in, ln_g, ln_b, w_mix, b_mix, w_out, *[w for w, _ in casts])


def _sgu_mix_tables(w_s, b_s, seq_len):
    length = min(seq_len, SGU_CHUNK)
    blk = np.arange(SGU_CHUNK) // RET_CHUNK
    w = jnp.where(blk[None, :] <= blk[:, None], w_s, 0)[:, :length, :length]
    reps = ROW_TILE // length
    seg = np.arange(ROW_TILE) // length
    tile = jnp.where(seg[None, :] == seg[:, None], jnp.tile(w, (1, reps, reps)), 0)
    bias = jnp.tile(b_s[:, :length].T, (reps, 1))
    return tile.astype(BF16), bias


def _ffn_body(s, t, final_norm, x_ref, mod_ref, g_ref, wup_ref, cw_ref, cb_ref,
              wdn_ref, buf_ref, fg_ref, xo_ref, co_ref, carry_ref):
    tm = s * t
    x = x_ref[...]
    mod = mod_ref[...]
    h = _modulated(x, g_ref[...], mod[:, 3 * D_MODEL:4 * D_MODEL],
                   mod[:, 4 * D_MODEL:5 * D_MODEL], s, t).astype(BF16)
    if s == 1:
        @pl.when(pl.program_id(1) == 0)
        def _():
            carry_ref[...] = buf_ref[0]

    def up(c, w):
        return [_dot(h, wup_ref[:, base + c:base + c + w]) for base in (0, FFN_D)]

    def conv(a, lo, w):
        r1 = pltpu.roll(a, 1, 0)
        r2 = pltpu.roll(a, 2, 0)
        if s == 1:
            p0 = carry_ref[0:1, lo:lo + w]
            p1 = carry_ref[1:2, lo:lo + w]
            carry_ref[:, lo:lo + w] = a[tm - 2:, :]
            row = lax.broadcasted_iota(jnp.int32, (8, 1), 0)
            a1 = jnp.concatenate([jnp.where(row == 0, p1, r1[:8]), r1[8:]], axis=0)
            a2 = jnp.concatenate(
                [jnp.where(row == 0, p0, jnp.where(row == 1, p1, r2[:8])), r2[8:]], axis=0)
        else:
            p0 = _rows(buf_ref[:, 0, lo:lo + w], s, t)
            p1 = _rows(buf_ref[:, 1, lo:lo + w], s, t)
            co_ref[:, :, lo:lo + w] = a.reshape(s, t, w)[:, t - 2:, :]
            pos = lax.broadcasted_iota(jnp.int32, (tm, 1), 0) & (t - 1)
            a1 = jnp.where(pos == 0, p1, r1)
            a2 = jnp.where(pos == 0, p0, jnp.where(pos == 1, p1, r2))
        c = cb_ref[:, lo:lo + w] + a2 * cw_ref[0:1, lo:lo + w]
        c = c + a1 * cw_ref[1:2, lo:lo + w]
        return c + a * cw_ref[2:3, lo:lo + w]

    y = jnp.zeros((tm, D_MODEL), F32)
    nxt = up(*FFN_CHUNKS[0])
    for k, (c, w) in enumerate(FFN_CHUNKS):
        cur = nxt
        if k + 1 < len(FFN_CHUNKS):
            nxt = up(*FFN_CHUNKS[k + 1])
        gate = conv(cur[0], c, w)
        val = conv(cur[1], FFN_D + c, w)
        hid = (_silu(gate) * val).astype(BF16)
        y = y + _dot(hid, wdn_ref[c:c + w, :])
    if s == 1:
        co_ref[0] = carry_ref[...]
    xn = x + _rows(mod[:, 5 * D_MODEL:6 * D_MODEL], s, t) * y
    if final_norm:
        xn = _rmsnorm(xn, fg_ref[...])
    xo_ref[...] = xn


def _ffn(x, mod, g, w_up, conv_w, conv_b, w_down, buf, final_g, s, t, final_norm, name,
         casts=()):
    nb, r, _ = x.shape
    tm = s * t
    assert t & (t - 1) == 0 and t >= CONV_W - 1
    grid = (nb, r // tm)
    in_specs = [
        pl.BlockSpec((None, tm, D_MODEL), lambda i, j: (i, j, 0)),
        pl.BlockSpec((None, s, N_MOD * D_MODEL), lambda i, j: (i, 0, 0)),
        _resident((1, D_MODEL)),
        _resident(w_up.shape),
        _resident((CONV_W, 2 * FFN_D)),
        _resident((1, 2 * FFN_D)),
        _resident(w_down.shape),
        pl.BlockSpec((None, s, CONV_W - 1, 2 * FFN_D), lambda i, j: (i, 0, 0, 0)),
        _resident((1, D_MODEL)),
    ]
    out_specs = [
        pl.BlockSpec((None, tm, D_MODEL), lambda i, j: (i, j, 0)),
        pl.BlockSpec((None, s, CONV_W - 1, 2 * FFN_D), lambda i, j: (i, 0, 0, 0)),
    ]
    out_shape = [
        jax.ShapeDtypeStruct(x.shape, F32),
        jax.ShapeDtypeStruct(buf.shape, F32),
    ]
    cast_in, cast_out, cast_shape = _cast_specs(casts, grid)
    return pl.pallas_call(
        _with_casts(functools.partial(_ffn_body, s, t, final_norm),
                    len(in_specs), len(out_specs), len(casts)),
        grid=grid,
        in_specs=in_specs + cast_in,
        out_specs=out_specs + cast_out,
        out_shape=out_shape + cast_shape,
        scratch_shapes=[pltpu.VMEM((CONV_W - 1, 2 * FFN_D), F32)],
        compiler_params=_params(56),
        name=name,
    )(x, mod, g, w_up, conv_w, conv_b, w_down, buf, final_g, *[w for w, _ in casts])


def kernel(x_prompt, x_sample, state_ret, state_ffn_conv, c_prompt, c_sample,
           w_ada, b_ada, norm_mix_g, norm_ffn_g,
           ret_w_in, ret_gn_g, ret_w_out,
           sgu_w_in, sgu_ln_g, sgu_ln_b, sgu_w_s, sgu_b_s, sgu_w_out,
           ffn_w_up, ffn_conv_w, ffn_conv_b, ffn_w_down, final_g):
    bp, seq, _ = x_prompt.shape
    bs, dec, _ = x_sample.shape
    depth = w_ada.shape[0]
    ss = ROW_TILE // dec
    nbs = bs // ss

    mod = _ada(jnp.concatenate([c_prompt, c_sample], axis=0), w_ada, b_ada)
    mod_p = mod[:, :bp].reshape(depth, bp, 1, -1)
    mod_s = mod[:, bp:].reshape(depth, nbs, ss, -1)

    xp = x_prompt
    xs = x_sample.reshape(nbs, ROW_TILE, D_MODEL)
    fg = final_g.reshape(1, D_MODEL)
    zero_buf = jnp.zeros((bp, 1, CONV_W - 1, 2 * FFN_D), F32)

    def mixer_weights(layer):
        src = (ret_w_in, ret_w_out) if layer % 2 == 0 else (sgu_w_in, sgu_w_out)
        return [(w, layer // 2) for w in src]

    w_in, w_out = [w[j].astype(BF16) for w, j in mixer_weights(0)]

    ret_p, ret_s, conv_p, conv_s, sgu_s = [], [], [], [], []
    for i in range(depth):
        j = i // 2
        g_mix = norm_mix_g[i].reshape(1, D_MODEL)
        ffn_casts = [(ffn_w_up, i), (ffn_w_down, i)]
        if i % 2 == 0:
            gn_g = ret_gn_g[j].reshape(1, RET_V)
            cos_p, sin_p = _rope_tables(0, seq)
            xp, sp, w_up, w_down = _ret_prompt(
                xp, mod_p[i], g_mix, w_in, cos_p, sin_p,
                *_decay_tables(ROW_TILE, RET_CHUNK), gn_g, w_out, casts=ffn_casts)
            cos_s, sin_s = _rope_tables(PAST_LEN, dec)
            proj = _proj_sample(xs, mod_s[i], g_mix, w_in, ss, dec)
            gated, s_new = _ret_sample_core(
                proj.reshape(bs, dec, -1), state_ret[j], cos_s, sin_s,
                *_decay_tables(dec, min(dec, RET_CHUNK)), gn_g)
            xs = _out_sample(gated.reshape(nbs, ROW_TILE, RET_V), xs, mod_s[i],
                             w_out, ss, dec)
            ret_p.append(sp)
            ret_s.append(s_new)
        else:
            ln_g = sgu_ln_g[j].reshape(1, SGU_D)
            ln_b = sgu_ln_b[j].reshape(1, SGU_D)
            wm_p, bm_p = _sgu_mix_tables(sgu_w_s[j], sgu_b_s[j], seq)
            wm_s, bm_s = _sgu_mix_tables(sgu_w_s[j], sgu_b_s[j], dec)
            xp, w_up, w_down = _sgu(xp, mod_p[i], g_mix, w_in, ln_g, ln_b, wm_p, bm_p,
                                    w_out, 1, ROW_TILE, False, casts=ffn_casts)
            xs, vs = _sgu(xs, mod_s[i], g_mix, w_in, ln_g, ln_b, wm_s, bm_s, w_out,
                          ss, dec, True)
            sgu_s.append(vs.reshape(bs, dec, SGU_D))
        g_ffn = norm_ffn_g[i].reshape(1, D_MODEL)
        conv_b = ffn_conv_b[i].reshape(1, 2 * FFN_D)
        last = i == depth - 1
        xp, cp, *next_mixer = _ffn(
            xp, mod_p[i], g_ffn, w_up, ffn_conv_w[i], conv_b, w_down, zero_buf, fg,
            1, FFN_ROW_TILE, last, "ffn_prompt_%d" % i,
            casts=[] if last else mixer_weights(i + 1))
        xs, cs = _ffn(xs, mod_s[i], g_ffn, w_up, ffn_conv_w[i], conv_b, w_down,
                      state_ffn_conv[i].reshape(nbs, ss, CONV_W - 1, 2 * FFN_D),
                      fg, ss, dec, last, "ffn_sample_%d" % i)
        if not last:
            w_in, w_out = next_mixer
        conv_p.append(cp.reshape(bp, CONV_W - 1, 2 * FFN_D))
        conv_s.append(cs.reshape(bs, CONV_W - 1, 2 * FFN_D))

    return (xp, xs.reshape(bs, dec, D_MODEL), jnp.stack(ret_p), jnp.stack(ret_s),
            jnp.stack(conv_p), jnp.stack(conv_s), jnp.stack(sgu_s))
```

```python
import functools

import jax
import jax.numpy as jnp
import numpy as np
from jax import lax
from jax.experimental import pallas as pl
from jax.experimental.pallas import tpu as pltpu

D_MODEL = 1024
N_MOD = 6
RET_HEADS = 4
RET_DK = 256
RET_DV = 512
RET_QK = RET_HEADS * RET_DK
RET_V = RET_HEADS * RET_DV
RET_CHUNK = 64
ROPE_BASE = 10000.0
PAST_LEN = 1024
SGU_CHUNK = 128
SGU_GROUPS = 4
SGU_D = 3 * D_MODEL
SGU_DG = SGU_D // SGU_GROUPS
FFN_D = 2816
CONV_W = 3
EPS = 1e-6

ROW_TILE = 256
FFN_ROW_TILE = 512
RET_SAMPLE_STREAMS = 4
FFN_CHUNKS = ((0, 1024), (1024, 1024), (2048, 768))
MIB = 1024 * 1024
LANES = 128
BF16_SUBLANES = 16

BF16 = jnp.bfloat16
F32 = jnp.float32


def _dot(a, b):
    return jnp.dot(a, b, preferred_element_type=F32)


def _resident(shape):
    nd = len(shape)
    return pl.BlockSpec(shape, lambda *_: (0,) * nd, pipeline_mode=pl.Buffered(1))


def _padded_cols(cols):
    return cols + LANES if cols % (8 * LANES) == 0 else cols


def _to_resident_bf16(w):
    rows, cols = w.shape
    return jnp.pad(w.astype(BF16), ((0, 0), (0, _padded_cols(cols) - cols)))


def _cast_specs(jobs, grid):
    steps = grid[0] * grid[1]
    in_specs, out_specs, out_shape = [], [], []
    for w, layer in jobs:
        _, rows, cols = w.shape
        parts = max(p for p in range(1, steps + 1)
                    if rows % p == 0 and (rows // p) % BF16_SUBLANES == 0)

        def slab(i, j, parts=parts):
            return jnp.minimum(i * grid[1] + j, parts - 1)

        in_specs.append(pl.BlockSpec((None, rows // parts, cols),
                                     lambda i, j, layer=layer, slab=slab: (layer, slab(i, j), 0)))
        out_specs.append(pl.BlockSpec((rows // parts, _padded_cols(cols)),
                                      lambda i, j, slab=slab: (slab(i, j), 0)))
        out_shape.append(jax.ShapeDtypeStruct((rows, _padded_cols(cols)), BF16))
    return in_specs, out_specs, out_shape


def _with_casts(body, n_in, n_out, n_cast):
    def wrapped(*refs):
        ins, refs = refs[:n_in], refs[n_in:]
        cast_in, refs = refs[:n_cast], refs[n_cast:]
        outs, refs = refs[:n_out], refs[n_out:]
        cast_out, scratch = refs[:n_cast], refs[n_cast:]
        for src, dst in zip(cast_in, cast_out):
            rows, cols = src.shape
            dst[:, :cols] = src[...].astype(BF16)
            if dst.shape[1] > cols:
                dst[:, cols:] = jnp.zeros((rows, dst.shape[1] - cols), BF16)
        body(*ins, *outs, *scratch)
    return wrapped


def _params(vmem_mib):
    return pltpu.CompilerParams(
        dimension_semantics=("arbitrary", "arbitrary"),
        vmem_limit_bytes=vmem_mib * MIB)


def _rows(v, s, t):
    if s == 1:
        return v
    return jnp.broadcast_to(v[:, None, :], (s, t, v.shape[-1])).reshape(s * t, v.shape[-1])


def _rmsnorm(x, g):
    ms = jnp.mean(x * x, axis=-1, keepdims=True)
    return x * lax.rsqrt(ms + EPS) * g


def _modulated(x, g, shift, scale, s, t):
    return _rmsnorm(x, g) * (1.0 + _rows(scale, s, t)) + _rows(shift, s, t)


def _silu(x):
    return x * jax.nn.sigmoid(x)


def _gelu(x):
    c = 2.0 * (2.0 / jnp.pi) ** 0.5
    return x * jax.nn.sigmoid(x * (c + (c * 0.044715) * (x * x)))


def _layernorm_nogain(o):
    mu = jnp.mean(o, axis=-1, keepdims=True)
    d = o - mu
    var = jnp.mean(d * d, axis=-1, keepdims=True)
    return d * lax.rsqrt(var + EPS)


def _ada_body(c_ref, w_ref, b_ref, o_ref):
    a = _silu(c_ref[...]).astype(BF16)
    o_ref[...] = _dot(a, w_ref[...].astype(BF16)) + b_ref[...]


def _ada(c_all, w_ada, b_ada):
    depth = w_ada.shape[0]
    n = c_all.shape[0]
    return pl.pallas_call(
        _ada_body,
        grid=(depth, N_MOD),
        in_specs=[
            pl.BlockSpec((n, D_MODEL), lambda i, j: (0, 0)),
            pl.BlockSpec((None, D_MODEL, D_MODEL), lambda i, j: (i, 0, j)),
            pl.BlockSpec((None, 1, D_MODEL), lambda i, j: (i, 0, j)),
        ],
        out_specs=pl.BlockSpec((None, n, D_MODEL), lambda i, j: (i, 0, j)),
        out_shape=jax.ShapeDtypeStruct((depth, n, N_MOD * D_MODEL), F32),
        compiler_params=_params(32),
        name="ada",
    )(c_all, w_ada, b_ada.reshape(depth, 1, N_MOD * D_MODEL))


def _rotary(x, cos, sin):
    half = x.shape[-1] // 2
    x1, x2 = x[:, :half], x[:, half:]
    return jnp.concatenate([x1 * cos - x2 * sin, x1 * sin + x2 * cos], axis=-1)


def _ret_head(q, k, v, s_prev, cos, sin, mask, qdec, kdec, gs):
    qb = (_rotary(q, cos, sin) * (RET_DK ** -0.5)).astype(BF16)
    kr = _rotary(k, cos, sin)
    kb = kr.astype(BF16)
    vb = v.astype(BF16)
    scores = lax.dot_general(qb, kb, (((1,), (1,)), ((), ())),
                             preferred_element_type=F32) * mask
    o = _dot(scores.astype(BF16), vb) + qdec * _dot(qb, s_prev.astype(BF16))
    kd = (kr * kdec).T.astype(BF16)
    s_new = gs * s_prev + _dot(kd, vb)
    return o, s_new


def _ret_prompt_body(x_ref, mod_ref, g_ref, win_ref, cos_ref, sin_ref, mask_ref,
                     qdec_ref, kdec_ref, gs_ref, gng_ref, wout_ref,
                     xo_ref, so_ref, s_ref):
    t = pl.program_id(1)

    @pl.when(t == 0)
    def _():
        s_ref[...] = jnp.zeros_like(s_ref)

    x = x_ref[...]
    mod = mod_ref[...]
    h = _modulated(x, g_ref[...], mod[:, 0:D_MODEL], mod[:, D_MODEL:2 * D_MODEL],
                   1, ROW_TILE).astype(BF16)
    cos = cos_ref[...]
    sin = sin_ref[...]
    y = jnp.zeros((ROW_TILE, D_MODEL), F32)
    for hd in range(RET_HEADS):
        q = _dot(h, win_ref[:, hd * RET_DK:(hd + 1) * RET_DK])
        k = _dot(h, win_ref[:, RET_QK + hd * RET_DK:RET_QK + (hd + 1) * RET_DK])
        v = _dot(h, win_ref[:, 2 * RET_QK + hd * RET_DV:2 * RET_QK + (hd + 1) * RET_DV])
        gate = _dot(h, win_ref[:, 2 * RET_QK + RET_V + hd * RET_DV:
                               2 * RET_QK + RET_V + (hd + 1) * RET_DV])
        o, s_new = _ret_head(q, k, v, s_ref[hd], cos, sin, mask_ref[hd],
                             qdec_ref[:, hd:hd + 1], kdec_ref[:, hd:hd + 1],
                             gs_ref[:, hd:hd + 1])
        s_ref[hd] = s_new
        on = _layernorm_nogain(o) * gng_ref[:, hd * RET_DV:(hd + 1) * RET_DV]
        gated = (_silu(gate) * on).astype(BF16)
        y = y + _dot(gated, wout_ref[hd * RET_DV:(hd + 1) * RET_DV, :D_MODEL])
    xo_ref[...] = x + mod[:, 2 * D_MODEL:3 * D_MODEL] * y

    @pl.when(t == pl.num_programs(1) - 1)
    def _():
        so_ref[...] = s_ref[...]


def _ret_prompt(x, mod, g, w_in, cos, sin, mask, qdec, kdec, gs, gn_g, w_out, casts=()):
    b, seq, _ = x.shape
    grid = (b, seq // ROW_TILE)
    in_specs = [
        pl.BlockSpec((None, ROW_TILE, D_MODEL), lambda i, t: (i, t, 0)),
        pl.BlockSpec((None, 1, N_MOD * D_MODEL), lambda i, t: (i, 0, 0)),
        _resident((1, D_MODEL)),
        _resident(w_in.shape),
        pl.BlockSpec((ROW_TILE, RET_DK // 2), lambda i, t: (t, 0)),
        pl.BlockSpec((ROW_TILE, RET_DK // 2), lambda i, t: (t, 0)),
        _resident(mask.shape),
        _resident(qdec.shape),
        _resident(kdec.shape),
        _resident(gs.shape),
        _resident((1, RET_V)),
        _resident(w_out.shape),
    ]
    out_specs = [
        pl.BlockSpec((None, ROW_TILE, D_MODEL), lambda i, t: (i, t, 0)),
        pl.BlockSpec((None, RET_HEADS, RET_DK, RET_DV), lambda i, t: (i, 0, 0, 0)),
    ]
    out_shape = [
        jax.ShapeDtypeStruct(x.shape, F32),
        jax.ShapeDtypeStruct((b, RET_HEADS, RET_DK, RET_DV), F32),
    ]
    cast_in, cast_out, cast_shape = _cast_specs(casts, grid)
    return pl.pallas_call(
        _with_casts(_ret_prompt_body, len(in_specs), len(out_specs), len(casts)),
        grid=grid,
        in_specs=in_specs + cast_in,
        out_specs=out_specs + cast_out,
        out_shape=out_shape + cast_shape,
        scratch_shapes=[pltpu.VMEM((RET_HEADS, RET_DK, RET_DV), F32)],
        compiler_params=_params(56),
        name="ret_prompt",
    )(x, mod.reshape(b, 1, -1), g, w_in, cos, sin, mask, qdec, kdec, gs, gn_g, w_out,
      *[w for w, _ in casts])


def _proj_body(s, t, x_ref, mod_ref, g_ref, w_ref, o_ref):
    mod = mod_ref[...]
    h = _modulated(x_ref[...], g_ref[...], mod[:, 0:D_MODEL],
                   mod[:, D_MODEL:2 * D_MODEL], s, t).astype(BF16)
    o_ref[...] = _dot(h, w_ref[:, :o_ref.shape[-1]])


def _proj_sample(x, mod, g, w, n, s, t):
    nb, tm, _ = x.shape
    return pl.pallas_call(
        functools.partial(_proj_body, s, t),
        grid=(nb, 1),
        in_specs=[
            pl.BlockSpec((None, tm, D_MODEL), lambda i, j: (i, 0, 0)),
            pl.BlockSpec((None, s, N_MOD * D_MODEL), lambda i, j: (i, 0, 0)),
            _resident((1, D_MODEL)),
            _resident(w.shape),
        ],
        out_specs=pl.BlockSpec((None, tm, n), lambda i, j: (i, 0, 0)),
        out_shape=jax.ShapeDtypeStruct((nb, tm, n), F32),
        compiler_params=_params(48),
        name="ret_sample_proj",
    )(x, mod, g, w)


def _ret_sample_core_body(p_ref, s0_ref, cos_ref, sin_ref, mask_ref, qdec_ref,
                          kdec_ref, gs_ref, gng_ref, o_ref, so_ref):
    cos = cos_ref[...]
    sin = sin_ref[...]
    for b in range(RET_SAMPLE_STREAMS):
        for hd in range(RET_HEADS):
            q = p_ref[b, :, hd * RET_DK:(hd + 1) * RET_DK]
            k = p_ref[b, :, RET_QK + hd * RET_DK:RET_QK + (hd + 1) * RET_DK]
            v = p_ref[b, :, 2 * RET_QK + hd * RET_DV:2 * RET_QK + (hd + 1) * RET_DV]
            gate = p_ref[b, :, 2 * RET_QK + RET_V + hd * RET_DV:
                         2 * RET_QK + RET_V + (hd + 1) * RET_DV]
            o, s_new = _ret_head(q, k, v, s0_ref[b, hd], cos, sin, mask_ref[hd],
                                 qdec_ref[:, hd:hd + 1], kdec_ref[:, hd:hd + 1],
                                 gs_ref[:, hd:hd + 1])
            so_ref[b, hd] = s_new
            on = _layernorm_nogain(o) * gng_ref[:, hd * RET_DV:(hd + 1) * RET_DV]
            o_ref[b, :, hd * RET_DV:(hd + 1) * RET_DV] = _silu(gate) * on


def _ret_sample_core(proj, s0, cos, sin, mask, qdec, kdec, gs, gn_g):
    b, t, n = proj.shape
    nb = RET_SAMPLE_STREAMS
    return pl.pallas_call(
        _ret_sample_core_body,
        grid=(b // nb, 1),
        in_specs=[
            pl.BlockSpec((nb, t, n), lambda i, j: (i, 0, 0)),
            pl.BlockSpec((nb, RET_HEADS, RET_DK, RET_DV), lambda i, j: (i, 0, 0, 0)),
            _resident(cos.shape),
            _resident(sin.shape),
            _resident(mask.shape),
            _resident(qdec.shape),
            _resident(kdec.shape),
            _resident(gs.shape),
            _resident((1, RET_V)),
        ],
        out_specs=[
            pl.BlockSpec((nb, t, RET_V), lambda i, j: (i, 0, 0)),
            pl.BlockSpec((nb, RET_HEADS, RET_DK, RET_DV), lambda i, j: (i, 0, 0, 0)),
        ],
        out_shape=[
            jax.ShapeDtypeStruct((b, t, RET_V), F32),
            jax.ShapeDtypeStruct(s0.shape, F32),
        ],
        compiler_params=_params(48),
        name="ret_sample_core",
    )(proj, s0, cos, sin, mask, qdec, kdec, gs, gn_g)


def _out_body(s, t, a_ref, x_ref, mod_ref, w_ref, o_ref):
    y = _dot(a_ref[...].astype(BF16), w_ref[:, :D_MODEL])
    gate = _rows(mod_ref[...][:, 2 * D_MODEL:3 * D_MODEL], s, t)
    o_ref[...] = x_ref[...] + gate * y


def _out_sample(a, x, mod, w, s, t):
    nb, tm, k = a.shape
    return pl.pallas_call(
        functools.partial(_out_body, s, t),
        grid=(nb, 1),
        in_specs=[
            pl.BlockSpec((None, tm, k), lambda i, j: (i, 0, 0)),
            pl.BlockSpec((None, tm, D_MODEL), lambda i, j: (i, 0, 0)),
            pl.BlockSpec((None, s, N_MOD * D_MODEL), lambda i, j: (i, 0, 0)),
            _resident(w.shape),
        ],
        out_specs=pl.BlockSpec((None, tm, D_MODEL), lambda i, j: (i, 0, 0)),
        out_shape=jax.ShapeDtypeStruct(x.shape, F32),
        compiler_params=_params(32),
        name="ret_sample_out",
    )(a, x, mod, w)


def _decay_tables(block, chunk):
    log_gamma = np.log(1.0 - np.exp2(-5.0 - np.arange(RET_HEADS)))
    idx = np.arange(block, dtype=np.float64)
    ch = np.arange(block) // chunk
    dist = np.abs(idx[:, None] - idx[None, :])
    decay = np.exp(log_gamma[:, None, None] * dist[None])
    mask = np.where((ch[None, :] <= ch[:, None])[None], decay, 0.0)
    qdec = np.exp((idx[:, None] + 1.0) * log_gamma[None, :])
    kdec = np.exp((block - 1.0 - idx)[:, None] * log_gamma[None, :])
    gs = np.exp(block * log_gamma)[None, :]
    return tuple(jnp.asarray(a, F32) for a in (mask, qdec, kdec, gs))


def _rope_tables(first, count):
    half = RET_DK // 2
    inv = np.power(ROPE_BASE, -np.arange(half, dtype=np.float64) / half)
    ang = (first + np.arange(count, dtype=np.float64))[:, None] * inv[None, :]
    return jnp.asarray(np.cos(ang), F32), jnp.asarray(np.sin(ang), F32)


def _sgu_body(s, t, emit_v, x_ref, mod_ref, g_ref, win_ref, lng_ref, lnb_ref,
              wmix_ref, bmix_ref, wout_ref, xo_ref, *maybe_v_ref):
    x = x_ref[...]
    mod = mod_ref[...]
    h = _modulated(x, g_ref[...], mod[:, 0:D_MODEL], mod[:, D_MODEL:2 * D_MODEL],
                   s, t).astype(BF16)
    groups = range(SGU_GROUPS)

    def cols(grp, base):
        return slice(base + grp * SGU_DG, base + (grp + 1) * SGU_DG)

    def proj(grp, base):
        return _dot(h, win_ref[:, cols(grp, base)])

    pv = [proj(0, SGU_D)]
    v = []
    for grp in groups:
        pv.append(proj(grp + 1, SGU_D) if grp + 1 < SGU_GROUPS else proj(0, 0))
        v.append(_gelu(pv[grp]))
    pu = [pv.pop(), proj(1, 0)]
    mu = sum(jnp.sum(vg, axis=-1, keepdims=True) for vg in v) * (1.0 / SGU_D)
    d = [vg - mu for vg in v]
    var = sum(jnp.sum(dg * dg, axis=-1, keepdims=True) for dg in d) * (1.0 / SGU_D)
    rs = lax.rsqrt(var + EPS)
    vb = []
    for grp in groups:
        vn = d[grp] * rs * lng_ref[:, cols(grp, 0)] + lnb_ref[:, cols(grp, 0)]
        if emit_v:
            maybe_v_ref[0][:, cols(grp, 0)] = vn
        vb.append(vn.astype(BF16))
    y = jnp.zeros((s * t, D_MODEL), F32)
    mixed = _dot(wmix_ref[0], vb[0])
    for grp in groups:
        if grp + 2 < SGU_GROUPS:
            pu.append(proj(grp + 2, 0))
        u = _gelu(pu[grp])
        cur = mixed
        if grp + 1 < SGU_GROUPS:
            mixed = _dot(wmix_ref[grp + 1], vb[grp + 1])
        gated = (u * (cur + bmix_ref[:, grp:grp + 1])).astype(BF16)
        y = y + _dot(gated, wout_ref[cols(grp, 0), :D_MODEL])
    xo_ref[...] = x + _rows(mod[:, 2 * D_MODEL:3 * D_MODEL], s, t) * y


def _sgu(x, mod, g, w_in, ln_g, ln_b, w_mix, b_mix, w_out, s, t, emit_v, casts=()):
    nb, r, _ = x.shape
    tm = s * t
    grid = (nb, r // tm)
    in_specs = [
        pl.BlockSpec((None, tm, D_MODEL), lambda i, j: (i, j, 0)),
        pl.BlockSpec((None, s, N_MOD * D_MODEL), lambda i, j: (i, 0, 0)),
        _resident((1, D_MODEL)),
        _resident(w_in.shape),
        _resident((1, SGU_D)),
        _resident((1, SGU_D)),
        _resident(w_mix.shape),
        _resident(b_mix.shape),
        _resident(w_out.shape),
    ]
    out_specs = [pl.BlockSpec((None, tm, D_MODEL), lambda i, j: (i, j, 0))]
    out_shape = [jax.ShapeDtypeStruct(x.shape, F32)]
    if emit_v:
        out_specs.append(pl.BlockSpec((None, tm, SGU_D), lambda i, j: (i, j, 0)))
        out_shape.append(jax.ShapeDtypeStruct((nb, r, SGU_D), F32))
    cast_in, cast_out, cast_shape = _cast_specs(casts, grid)
    return pl.pallas_call(
        _with_casts(functools.partial(_sgu_body, s, t, emit_v),
                    len(in_specs), len(out_specs), len(casts)),
        grid=grid,
        in_specs=in_specs + cast_in,
        out_specs=out_specs + cast_out,
        out_shape=out_shape + cast_shape,
        compiler_params=_params(56),
        name="sgu_sample" if emit_v else "sgu_prompt",
    )(x, mod, g, w_in, ln_g, ln_b, w_mix, b_mix, w_out, *[w for w, _ in casts])


def _sgu_mix_tables(w_s, b_s, seq_len):
    length = min(seq_len, SGU_CHUNK)
    blk = np.arange(SGU_CHUNK) // RET_CHUNK
    w = jnp.where(blk[None, :] <= blk[:, None], w_s, 0)[:, :length, :length]
    reps = ROW_TILE // length
    seg = np.arange(ROW_TILE) // length
    tile = jnp.where(seg[None, :] == seg[:, None], jnp.tile(w, (1, reps, reps)), 0)
    bias = jnp.tile(b_s[:, :length].T, (reps, 1))
    return tile.astype(BF16), bias


def _ffn_body(s, t, final_norm, x_ref, mod_ref, g_ref, wup_ref, cw_ref, cb_ref,
              wdn_ref, buf_ref, fg_ref, xo_ref, co_ref, carry_ref):
    tm = s * t
    x = x_ref[...]
    mod = mod_ref[...]
    h = _modulated(x, g_ref[...], mod[:, 3 * D_MODEL:4 * D_MODEL],
                   mod[:, 4 * D_MODEL:5 * D_MODEL], s, t).astype(BF16)
    if s == 1:
        @pl.when(pl.program_id(1) == 0)
        def _():
            carry_ref[...] = buf_ref[0]

    def up(c, w):
        return [_dot(h, wup_ref[:, base + c:base + c + w]) for base in (0, FFN_D)]

    def conv(a, lo, w):
        r1 = pltpu.roll(a, 1, 0)
        r2 = pltpu.roll(a, 2, 0)
        if s == 1:
            p0 = carry_ref[0:1, lo:lo + w]
            p1 = carry_ref[1:2, lo:lo + w]
            carry_ref[:, lo:lo + w] = a[tm - 2:, :]
            row = lax.broadcasted_iota(jnp.int32, (8, 1), 0)
            a1 = jnp.concatenate([jnp.where(row == 0, p1, r1[:8]), r1[8:]], axis=0)
            a2 = jnp.concatenate(
                [jnp.where(row == 0, p0, jnp.where(row == 1, p1, r2[:8])), r2[8:]], axis=0)
        else:
            p0 = _rows(buf_ref[:, 0, lo:lo + w], s, t)
            p1 = _rows(buf_ref[:, 1, lo:lo + w], s, t)
            co_ref[:, :, lo:lo + w] = a.reshape(s, t, w)[:, t - 2:, :]
            pos = lax.broadcasted_iota(jnp.int32, (tm, 1), 0) & (t - 1)
            a1 = jnp.where(pos == 0, p1, r1)
            a2 = jnp.where(pos == 0, p0, jnp.where(pos == 1, p1, r2))
        c = cb_ref[:, lo:lo + w] + a2 * cw_ref[0:1, lo:lo + w]
        c = c + a1 * cw_ref[1:2, lo:lo + w]
        return c + a * cw_ref[2:3, lo:lo + w]

    y = jnp.zeros((tm, D_MODEL), F32)
    nxt = up(*FFN_CHUNKS[0])
    for k, (c, w) in enumerate(FFN_CHUNKS):
        cur = nxt
        if k + 1 < len(FFN_CHUNKS):
            nxt = up(*FFN_CHUNKS[k + 1])
        gate = conv(cur[0], c, w)
        val = conv(cur[1], FFN_D + c, w)
        hid = (_silu(gate) * val).astype(BF16)
        y = y + _dot(hid, wdn_ref[c:c + w, :D_MODEL])
    if s == 1:
        co_ref[0] = carry_ref[...]
    xn = x + _rows(mod[:, 5 * D_MODEL:6 * D_MODEL], s, t) * y
    if final_norm:
        xn = _rmsnorm(xn, fg_ref[...])
    xo_ref[...] = xn


def _ffn(x, mod, g, w_up, conv_w, conv_b, w_down, buf, final_g, s, t, final_norm, name,
         casts=()):
    nb, r, _ = x.shape
    tm = s * t
    assert t & (t - 1) == 0 and t >= CONV_W - 1
    grid = (nb, r // tm)
    in_specs = [
        pl.BlockSpec((None, tm, D_MODEL), lambda i, j: (i, j, 0)),
        pl.BlockSpec((None, s, N_MOD * D_MODEL), lambda i, j: (i, 0, 0)),
        _resident((1, D_MODEL)),
        _resident(w_up.shape),
        _resident((CONV_W, 2 * FFN_D)),
        _resident((1, 2 * FFN_D)),
        _resident(w_down.shape),
        pl.BlockSpec((None, s, CONV_W - 1, 2 * FFN_D), lambda i, j: (i, 0, 0, 0)),
        _resident((1, D_MODEL)),
    ]
    out_specs = [
        pl.BlockSpec((None, tm, D_MODEL), lambda i, j: (i, j, 0)),
        pl.BlockSpec((None, s, CONV_W - 1, 2 * FFN_D), lambda i, j: (i, 0, 0, 0)),
    ]
    out_shape = [
        jax.ShapeDtypeStruct(x.shape, F32),
        jax.ShapeDtypeStruct(buf.shape, F32),
    ]
    cast_in, cast_out, cast_shape = _cast_specs(casts, grid)
    return pl.pallas_call(
        _with_casts(functools.partial(_ffn_body, s, t, final_norm),
                    len(in_specs), len(out_specs), len(casts)),
        grid=grid,
        in_specs=in_specs + cast_in,
        out_specs=out_specs + cast_out,
        out_shape=out_shape + cast_shape,
        scratch_shapes=[pltpu.VMEM((CONV_W - 1, 2 * FFN_D), F32)],
        compiler_params=_params(56),
        name=name,
    )(x, mod, g, w_up, conv_w, conv_b, w_down, buf, final_g, *[w for w, _ in casts])


def kernel(x_prompt, x_sample, state_ret, state_ffn_conv, c_prompt, c_sample,
           w_ada, b_ada, norm_mix_g, norm_ffn_g,
           ret_w_in, ret_gn_g, ret_w_out,
           sgu_w_in, sgu_ln_g, sgu_ln_b, sgu_w_s, sgu_b_s, sgu_w_out,
           ffn_w_up, ffn_conv_w, ffn_conv_b, ffn_w_down, final_g):
    bp, seq, _ = x_prompt.shape
    bs, dec, _ = x_sample.shape
    depth = w_ada.shape[0]
    ss = ROW_TILE // dec
    nbs = bs // ss

    mod = _ada(jnp.concatenate([c_prompt, c_sample], axis=0), w_ada, b_ada)
    mod_p = mod[:, :bp].reshape(depth, bp, 1, -1)
    mod_s = mod[:, bp:].reshape(depth, nbs, ss, -1)

    xp = x_prompt
    xs = x_sample.reshape(nbs, ROW_TILE, D_MODEL)
    fg = final_g.reshape(1, D_MODEL)
    zero_buf = jnp.zeros((bp, 1, CONV_W - 1, 2 * FFN_D), F32)

    def mixer_weights(layer):
        src = (ret_w_in, ret_w_out) if layer % 2 == 0 else (sgu_w_in, sgu_w_out)
        return [(w, layer // 2) for w in src]

    w_in, w_out = [_to_resident_bf16(w[j]) for w, j in mixer_weights(0)]

    ret_p, ret_s, conv_p, conv_s, sgu_s = [], [], [], [], []
    for i in range(depth):
        j = i // 2
        g_mix = norm_mix_g[i].reshape(1, D_MODEL)
        ffn_casts = [(ffn_w_up, i), (ffn_w_down, i)]
        if i % 2 == 0:
            gn_g = ret_gn_g[j].reshape(1, RET_V)
            cos_p, sin_p = _rope_tables(0, seq)
            xp, sp, w_up, w_down = _ret_prompt(
                xp, mod_p[i], g_mix, w_in, cos_p, sin_p,
                *_decay_tables(ROW_TILE, RET_CHUNK), gn_g, w_out, casts=ffn_casts)
            cos_s, sin_s = _rope_tables(PAST_LEN, dec)
            proj = _proj_sample(xs, mod_s[i], g_mix, w_in, 2 * RET_QK + 2 * RET_V, ss, dec)
            gated, s_new = _ret_sample_core(
                proj.reshape(bs, dec, -1), state_ret[j], cos_s, sin_s,
                *_decay_tables(dec, min(dec, RET_CHUNK)), gn_g)
            xs = _out_sample(gated.reshape(nbs, ROW_TILE, RET_V), xs, mod_s[i],
                             w_out, ss, dec)
            ret_p.append(sp)
            ret_s.append(s_new)
        else:
            ln_g = sgu_ln_g[j].reshape(1, SGU_D)
            ln_b = sgu_ln_b[j].reshape(1, SGU_D)
            wm_p, bm_p = _sgu_mix_tables(sgu_w_s[j], sgu_b_s[j], seq)
            wm_s, bm_s = _sgu_mix_tables(sgu_w_s[j], sgu_b_s[j], dec)
            xp, w_up, w_down = _sgu(xp, mod_p[i], g_mix, w_in, ln_g, ln_b, wm_p, bm_p,
                                    w_out, 1, ROW_TILE, False, casts=ffn_casts)
            xs, vs = _sgu(xs, mod_s[i], g_mix, w_in, ln_g, ln_b, wm_s, bm_s, w_out,
                          ss, dec, True)
            sgu_s.append(vs.reshape(bs, dec, SGU_D))
        g_ffn = norm_ffn_g[i].reshape(1, D_MODEL)
        conv_b = ffn_conv_b[i].reshape(1, 2 * FFN_D)
        last = i == depth - 1
        xp, cp, *next_mixer = _ffn(
            xp, mod_p[i], g_ffn, w_up, ffn_conv_w[i], conv_b, w_down, zero_buf, fg,
            1, FFN_ROW_TILE, last, "ffn_prompt_%d" % i,
            casts=[] if last else mixer_weights(i + 1))
        xs, cs = _ffn(xs, mod_s[i], g_ffn, w_up, ffn_conv_w[i], conv_b, w_down,
                      state_ffn_conv[i].reshape(nbs, ss, CONV_W - 1, 2 * FFN_D),
                      fg, ss, dec, last, "ffn_sample_%d" % i)
        if not last:
            w_in, w_out = next_mixer
        conv_p.append(cp.reshape(bp, CONV_W - 1, 2 * FFN_D))
        conv_s.append(cs.reshape(bs, CONV_W - 1, 2 * FFN_D))

    return (xp, xs.reshape(bs, dec, D_MODEL), jnp.stack(ret_p), jnp.stack(ret_s),
            jnp.stack(conv_p), jnp.stack(conv_s), jnp.stack(sgu_s))
```

```python
import functools

import jax
import jax.numpy as jnp
import numpy as np
from jax import lax
from jax.experimental import pallas as pl
from jax.experimental.pallas import tpu as pltpu

D_MODEL = 1024
N_MOD = 6
RET_HEADS = 4
RET_DK = 256
RET_DV = 512
RET_QK = RET_HEADS * RET_DK
RET_V = RET_HEADS * RET_DV
RET_CHUNK = 64
ROPE_BASE = 10000.0
PAST_LEN = 1024
SGU_CHUNK = 128
SGU_GROUPS = 4
SGU_D = 3 * D_MODEL
SGU_DG = SGU_D // SGU_GROUPS
FFN_D = 2816
CONV_W = 3
EPS = 1e-6

ROW_TILE = 256
FFN_ROW_TILE = 512
MIXER_ROW_TILE = 512
RET_SAMPLE_STREAMS = 4
FFN_CHUNKS = ((0, 1024), (1024, 1024), (2048, 768))
MIB = 1024 * 1024
LANES = 128
LOG2_E = float(np.log2(np.e))
BF16_SUBLANES = 16

BF16 = jnp.bfloat16
F32 = jnp.float32


def _dot(a, b):
    return jnp.dot(a, b, preferred_element_type=F32)


def _resident(shape):
    nd = len(shape)
    return pl.BlockSpec(shape, lambda *_: (0,) * nd, pipeline_mode=pl.Buffered(1))


def _padded_cols(cols):
    return cols + LANES if cols % (8 * LANES) == 0 else cols


def _to_resident_bf16(w):
    rows, cols = w.shape
    return jnp.pad(w.astype(BF16), ((0, 0), (0, _padded_cols(cols) - cols)))


def _cast_specs(jobs, grid):
    steps = grid[0] * grid[1]
    in_specs, out_specs, out_shape = [], [], []
    for w, layer in jobs:
        _, rows, cols = w.shape
        parts = max(p for p in range(1, steps + 1)
                    if rows % p == 0 and (rows // p) % BF16_SUBLANES == 0)

        def slab(i, j, parts=parts):
            return jnp.minimum(i * grid[1] + j, parts - 1)

        in_specs.append(pl.BlockSpec((None, rows // parts, cols),
                                     lambda i, j, layer=layer, slab=slab: (layer, slab(i, j), 0)))
        out_specs.append(pl.BlockSpec((rows // parts, _padded_cols(cols)),
                                      lambda i, j, slab=slab: (slab(i, j), 0)))
        out_shape.append(jax.ShapeDtypeStruct((rows, _padded_cols(cols)), BF16))
    return in_specs, out_specs, out_shape


def _with_casts(body, n_in, n_out, n_cast):
    def wrapped(*refs):
        ins, refs = refs[:n_in], refs[n_in:]
        cast_in, refs = refs[:n_cast], refs[n_cast:]
        outs, refs = refs[:n_out], refs[n_out:]
        cast_out, scratch = refs[:n_cast], refs[n_cast:]
        for src, dst in zip(cast_in, cast_out):
            rows, cols = src.shape
            dst[:, :cols] = src[...].astype(BF16)
            if dst.shape[1] > cols:
                dst[:, cols:] = jnp.zeros((rows, dst.shape[1] - cols), BF16)
        body(*ins, *outs, *scratch)
    return wrapped


def _params(vmem_mib):
    return pltpu.CompilerParams(
        dimension_semantics=("arbitrary", "arbitrary"),
        vmem_limit_bytes=vmem_mib * MIB)


def _rows(v, s, t):
    if s == 1:
        return v
    return jnp.broadcast_to(v[:, None, :], (s, t, v.shape[-1])).reshape(s * t, v.shape[-1])


def _rmsnorm(x, g):
    ms = jnp.mean(x * x, axis=-1, keepdims=True)
    return x * lax.rsqrt(ms + EPS) * g


def _modulated(x, g, shift, scale, s, t):
    return _rmsnorm(x, g) * (1.0 + _rows(scale, s, t)) + _rows(shift, s, t)


def _silu(x):
    return x / (1.0 + jnp.exp2(x * -LOG2_E))


def _gelu(x):
    k = -2.0 * (2.0 / np.pi) ** 0.5 * LOG2_E
    return x / (1.0 + jnp.exp2(x * (k + (k * 0.044715) * (x * x))))


def _layernorm_nogain(o):
    mu = jnp.mean(o, axis=-1, keepdims=True)
    d = o - mu
    var = jnp.mean(d * d, axis=-1, keepdims=True)
    return d * lax.rsqrt(var + EPS)


def _ada_body(c_ref, w_ref, b_ref, o_ref):
    a = _silu(c_ref[...]).astype(BF16)
    o_ref[...] = _dot(a, w_ref[...].astype(BF16)) + b_ref[...]


def _ada(c_all, w_ada, b_ada):
    depth = w_ada.shape[0]
    n = c_all.shape[0]
    return pl.pallas_call(
        _ada_body,
        grid=(depth, N_MOD),
        in_specs=[
            pl.BlockSpec((n, D_MODEL), lambda i, j: (0, 0)),
            pl.BlockSpec((None, D_MODEL, D_MODEL), lambda i, j: (i, 0, j)),
            pl.BlockSpec((None, 1, D_MODEL), lambda i, j: (i, 0, j)),
        ],
        out_specs=pl.BlockSpec((None, n, D_MODEL), lambda i, j: (i, 0, j)),
        out_shape=jax.ShapeDtypeStruct((depth, n, N_MOD * D_MODEL), F32),
        compiler_params=_params(32),
        name="ada",
    )(c_all, w_ada, b_ada.reshape(depth, 1, N_MOD * D_MODEL))


def _rotary(x, cos, sin):
    half = x.shape[-1] // 2
    x1, x2 = x[:, :half], x[:, half:]
    return jnp.concatenate([x1 * cos - x2 * sin, x1 * sin + x2 * cos], axis=-1)


def _ret_head(q, k, v, s_prev, cos, sin, mask, qdec, kdec, gs):
    qb = (_rotary(q, cos, sin) * (RET_DK ** -0.5)).astype(BF16)
    kr = _rotary(k, cos, sin)
    kb = kr.astype(BF16)
    vb = v.astype(BF16)
    scores = lax.dot_general(qb, kb, (((1,), (1,)), ((), ())),
                             preferred_element_type=F32) * mask
    o = _dot(scores.astype(BF16), vb) + qdec * _dot(qb, s_prev.astype(BF16))
    kd = (kr * kdec).T.astype(BF16)
    s_new = gs * s_prev + _dot(kd, vb)
    return o, s_new


def _ret_prompt_body(x_ref, mod_ref, g_ref, win_ref, cos_ref, sin_ref, mask_ref,
                     qdec_ref, kdec_ref, gs_ref, gng_ref, wout_ref,
                     xo_ref, so_ref, s_ref):
    t = pl.program_id(1)

    @pl.when(t == 0)
    def _():
        s_ref[...] = jnp.zeros_like(s_ref)

    x = x_ref[...]
    mod = mod_ref[...]
    tm = x.shape[0]
    h = _modulated(x, g_ref[...], mod[:, 0:D_MODEL], mod[:, D_MODEL:2 * D_MODEL],
                   1, tm).astype(BF16)
    y = jnp.zeros((tm, D_MODEL), F32)
    for hd in range(RET_HEADS):
        q = _dot(h, win_ref[:, hd * RET_DK:(hd + 1) * RET_DK])
        k = _dot(h, win_ref[:, RET_QK + hd * RET_DK:RET_QK + (hd + 1) * RET_DK])
        v = _dot(h, win_ref[:, 2 * RET_QK + hd * RET_DV:2 * RET_QK + (hd + 1) * RET_DV])
        gate = _dot(h, win_ref[:, 2 * RET_QK + RET_V + hd * RET_DV:
                               2 * RET_QK + RET_V + (hd + 1) * RET_DV])
        state = s_ref[hd]
        blocks = []
        for r in range(0, tm, ROW_TILE):
            rows = slice(r, r + ROW_TILE)
            o, state = _ret_head(q[rows], k[rows], v[rows], state,
                                 cos_ref[rows, :], sin_ref[rows, :], mask_ref[hd],
                                 qdec_ref[:, hd:hd + 1], kdec_ref[:, hd:hd + 1],
                                 gs_ref[:, hd:hd + 1])
            blocks.append(o)
        s_ref[hd] = state
        o = blocks[0] if len(blocks) == 1 else jnp.concatenate(blocks, axis=0)
        on = _layernorm_nogain(o) * gng_ref[:, hd * RET_DV:(hd + 1) * RET_DV]
        gated = (_silu(gate) * on).astype(BF16)
        y = y + _dot(gated, wout_ref[hd * RET_DV:(hd + 1) * RET_DV, :D_MODEL])
    xo_ref[...] = x + mod[:, 2 * D_MODEL:3 * D_MODEL] * y

    @pl.when(t == pl.num_programs(1) - 1)
    def _():
        so_ref[...] = s_ref[...]


def _ret_prompt(x, mod, g, w_in, cos, sin, mask, qdec, kdec, gs, gn_g, w_out, casts=()):
    b, seq, _ = x.shape
    tm = MIXER_ROW_TILE
    grid = (b, seq // tm)
    in_specs = [
        pl.BlockSpec((None, tm, D_MODEL), lambda i, t: (i, t, 0)),
        pl.BlockSpec((None, 1, N_MOD * D_MODEL), lambda i, t: (i, 0, 0)),
        _resident((1, D_MODEL)),
        _resident(w_in.shape),
        pl.BlockSpec((tm, RET_DK // 2), lambda i, t: (t, 0)),
        pl.BlockSpec((tm, RET_DK // 2), lambda i, t: (t, 0)),
        _resident(mask.shape),
        _resident(qdec.shape),
        _resident(kdec.shape),
        _resident(gs.shape),
        _resident((1, RET_V)),
        _resident(w_out.shape),
    ]
    out_specs = [
        pl.BlockSpec((None, tm, D_MODEL), lambda i, t: (i, t, 0)),
        pl.BlockSpec((None, RET_HEADS, RET_DK, RET_DV), lambda i, t: (i, 0, 0, 0)),
    ]
    out_shape = [
        jax.ShapeDtypeStruct(x.shape, F32),
        jax.ShapeDtypeStruct((b, RET_HEADS, RET_DK, RET_DV), F32),
    ]
    cast_in, cast_out, cast_shape = _cast_specs(casts, grid)
    return pl.pallas_call(
        _with_casts(_ret_prompt_body, len(in_specs), len(out_specs), len(casts)),
        grid=grid,
        in_specs=in_specs + cast_in,
        out_specs=out_specs + cast_out,
        out_shape=out_shape + cast_shape,
        scratch_shapes=[pltpu.VMEM((RET_HEADS, RET_DK, RET_DV), F32)],
        compiler_params=_params(56),
        name="ret_prompt",
    )(x, mod.reshape(b, 1, -1), g, w_in, cos, sin, mask, qdec, kdec, gs, gn_g, w_out,
      *[w for w, _ in casts])


def _proj_body(s, t, x_ref, mod_ref, g_ref, w_ref, o_ref):
    mod = mod_ref[...]
    h = _modulated(x_ref[...], g_ref[...], mod[:, 0:D_MODEL],
                   mod[:, D_MODEL:2 * D_MODEL], s, t).astype(BF16)
    o_ref[...] = _dot(h, w_ref[:, :o_ref.shape[-1]])


def _proj_sample(x, mod, g, w, n, s, t):
    nb, tm, _ = x.shape
    return pl.pallas_call(
        functools.partial(_proj_body, s, t),
        grid=(nb, 1),
        in_specs=[
            pl.BlockSpec((None, tm, D_MODEL), lambda i, j: (i, 0, 0)),
            pl.BlockSpec((None, s, N_MOD * D_MODEL), lambda i, j: (i, 0, 0)),
            _resident((1, D_MODEL)),
            _resident(w.shape),
        ],
        out_specs=pl.BlockSpec((None, tm, n), lambda i, j: (i, 0, 0)),
        out_shape=jax.ShapeDtypeStruct((nb, tm, n), F32),
        compiler_params=_params(48),
        name="ret_sample_proj",
    )(x, mod, g, w)


def _ret_sample_core_body(p_ref, s0_ref, cos_ref, sin_ref, mask_ref, qdec_ref,
                          kdec_ref, gs_ref, gng_ref, o_ref, so_ref):
    cos = cos_ref[...]
    sin = sin_ref[...]
    for b in range(RET_SAMPLE_STREAMS):
        for hd in range(RET_HEADS):
            q = p_ref[b, :, hd * RET_DK:(hd + 1) * RET_DK]
            k = p_ref[b, :, RET_QK + hd * RET_DK:RET_QK + (hd + 1) * RET_DK]
            v = p_ref[b, :, 2 * RET_QK + hd * RET_DV:2 * RET_QK + (hd + 1) * RET_DV]
            gate = p_ref[b, :, 2 * RET_QK + RET_V + hd * RET_DV:
                         2 * RET_QK + RET_V + (hd + 1) * RET_DV]
            o, s_new = _ret_head(q, k, v, s0_ref[b, hd], cos, sin, mask_ref[hd],
                                 qdec_ref[:, hd:hd + 1], kdec_ref[:, hd:hd + 1],
                                 gs_ref[:, hd:hd + 1])
            so_ref[b, hd] = s_new
            on = _layernorm_nogain(o) * gng_ref[:, hd * RET_DV:(hd + 1) * RET_DV]
            o_ref[b, :, hd * RET_DV:(hd + 1) * RET_DV] = _silu(gate) * on


def _ret_sample_core(proj, s0, cos, sin, mask, qdec, kdec, gs, gn_g):
    b, t, n = proj.shape
    nb = RET_SAMPLE_STREAMS
    return pl.pallas_call(
        _ret_sample_core_body,
        grid=(b // nb, 1),
        in_specs=[
            pl.BlockSpec((nb, t, n), lambda i, j: (i, 0, 0)),
            pl.BlockSpec((nb, RET_HEADS, RET_DK, RET_DV), lambda i, j: (i, 0, 0, 0)),
            _resident(cos.shape),
            _resident(sin.shape),
            _resident(mask.shape),
            _resident(qdec.shape),
            _resident(kdec.shape),
            _resident(gs.shape),
            _resident((1, RET_V)),
        ],
        out_specs=[
            pl.BlockSpec((nb, t, RET_V), lambda i, j: (i, 0, 0)),
            pl.BlockSpec((nb, RET_HEADS, RET_DK, RET_DV), lambda i, j: (i, 0, 0, 0)),
        ],
        out_shape=[
            jax.ShapeDtypeStruct((b, t, RET_V), F32),
            jax.ShapeDtypeStruct(s0.shape, F32),
        ],
        compiler_params=_params(48),
        name="ret_sample_core",
    )(proj, s0, cos, sin, mask, qdec, kdec, gs, gn_g)


def _out_body(s, t, a_ref, x_ref, mod_ref, w_ref, o_ref):
    y = _dot(a_ref[...].astype(BF16), w_ref[:, :D_MODEL])
    gate = _rows(mod_ref[...][:, 2 * D_MODEL:3 * D_MODEL], s, t)
    o_ref[...] = x_ref[...] + gate * y


def _out_sample(a, x, mod, w, s, t):
    nb, tm, k = a.shape
    return pl.pallas_call(
        functools.partial(_out_body, s, t),
        grid=(nb, 1),
        in_specs=[
            pl.BlockSpec((None, tm, k), lambda i, j: (i, 0, 0)),
            pl.BlockSpec((None, tm, D_MODEL), lambda i, j: (i, 0, 0)),
            pl.BlockSpec((None, s, N_MOD * D_MODEL), lambda i, j: (i, 0, 0)),
            _resident(w.shape),
        ],
        out_specs=pl.BlockSpec((None, tm, D_MODEL), lambda i, j: (i, 0, 0)),
        out_shape=jax.ShapeDtypeStruct(x.shape, F32),
        compiler_params=_params(32),
        name="ret_sample_out",
    )(a, x, mod, w)


def _decay_tables(block, chunk):
    log_gamma = np.log(1.0 - np.exp2(-5.0 - np.arange(RET_HEADS)))
    idx = np.arange(block, dtype=np.float64)
    ch = np.arange(block) // chunk
    dist = np.abs(idx[:, None] - idx[None, :])
    decay = np.exp(log_gamma[:, None, None] * dist[None])
    mask = np.where((ch[None, :] <= ch[:, None])[None], decay, 0.0)
    qdec = np.exp((idx[:, None] + 1.0) * log_gamma[None, :])
    kdec = np.exp((block - 1.0 - idx)[:, None] * log_gamma[None, :])
    gs = np.exp(block * log_gamma)[None, :]
    return tuple(jnp.asarray(a, F32) for a in (mask, qdec, kdec, gs))


def _rope_tables(first, count):
    half = RET_DK // 2
    inv = np.power(ROPE_BASE, -np.arange(half, dtype=np.float64) / half)
    ang = (first + np.arange(count, dtype=np.float64))[:, None] * inv[None, :]
    return jnp.asarray(np.cos(ang), F32), jnp.asarray(np.sin(ang), F32)


def _sgu_body(s, t, emit_v, x_ref, mod_ref, g_ref, win_ref, lng_ref, lnb_ref,
              wmix_ref, bmix_ref, wout_ref, xo_ref, *maybe_v_ref):
    x = x_ref[...]
    mod = mod_ref[...]
    h = _modulated(x, g_ref[...], mod[:, 0:D_MODEL], mod[:, D_MODEL:2 * D_MODEL],
                   s, t).astype(BF16)
    groups = range(SGU_GROUPS)

    def cols(grp, base):
        return slice(base + grp * SGU_DG, base + (grp + 1) * SGU_DG)

    def proj(grp, base):
        return _dot(h, win_ref[:, cols(grp, base)])

    pv = [proj(0, SGU_D)]
    v = []
    for grp in groups:
        pv.append(proj(grp + 1, SGU_D) if grp + 1 < SGU_GROUPS else proj(0, 0))
        v.append(_gelu(pv[grp]))
    pu = [pv.pop(), proj(1, 0)]
    mu = sum(jnp.sum(vg, axis=-1, keepdims=True) for vg in v) * (1.0 / SGU_D)
    d = [vg - mu for vg in v]
    var = sum(jnp.sum(dg * dg, axis=-1, keepdims=True) for dg in d) * (1.0 / SGU_D)
    rs = lax.rsqrt(var + EPS)
    vb = []
    for grp in groups:
        vn = d[grp] * rs * lng_ref[:, cols(grp, 0)] + lnb_ref[:, cols(grp, 0)]
        if emit_v:
            maybe_v_ref[0][:, cols(grp, 0)] = vn
        vb.append(vn.astype(BF16))
    def mix(grp):
        parts = [_dot(wmix_ref[grp], vb[grp][r:r + ROW_TILE]) + bmix_ref[:, grp:grp + 1]
                 for r in range(0, s * t, ROW_TILE)]
        return parts[0] if len(parts) == 1 else jnp.concatenate(parts, axis=0)

    y = jnp.zeros((s * t, D_MODEL), F32)
    mixed = mix(0)
    for grp in groups:
        if grp + 2 < SGU_GROUPS:
            pu.append(proj(grp + 2, 0))
        u = _gelu(pu[grp])
        cur = mixed
        if grp + 1 < SGU_GROUPS:
            mixed = mix(grp + 1)
        gated = (u * cur).astype(BF16)
        y = y + _dot(gated, wout_ref[cols(grp, 0), :D_MODEL])
    xo_ref[...] = x + _rows(mod[:, 2 * D_MODEL:3 * D_MODEL], s, t) * y


def _sgu(x, mod, g, w_in, ln_g, ln_b, w_mix, b_mix, w_out, s, t, emit_v, casts=()):
    nb, r, _ = x.shape
    tm = s * t
    grid = (nb, r // tm)
    in_specs = [
        pl.BlockSpec((None, tm, D_MODEL), lambda i, j: (i, j, 0)),
        pl.BlockSpec((None, s, N_MOD * D_MODEL), lambda i, j: (i, 0, 0)),
        _resident((1, D_MODEL)),
        _resident(w_in.shape),
        _resident((1, SGU_D)),
        _resident((1, SGU_D)),
        _resident(w_mix.shape),
        _resident(b_mix.shape),
        _resident(w_out.shape),
    ]
    out_specs = [pl.BlockSpec((None, tm, D_MODEL), lambda i, j: (i, j, 0))]
    out_shape = [jax.ShapeDtypeStruct(x.shape, F32)]
    if emit_v:
        out_specs.append(pl.BlockSpec((None, tm, SGU_D), lambda i, j: (i, j, 0)))
        out_shape.append(jax.ShapeDtypeStruct((nb, r, SGU_D), F32))
    cast_in, cast_out, cast_shape = _cast_specs(casts, grid)
    return pl.pallas_call(
        _with_casts(functools.partial(_sgu_body, s, t, emit_v),
                    len(in_specs), len(out_specs), len(casts)),
        grid=grid,
        in_specs=in_specs + cast_in,
        out_specs=out_specs + cast_out,
        out_shape=out_shape + cast_shape,
        compiler_params=_params(56),
        name="sgu_sample" if emit_v else "sgu_prompt",
    )(x, mod, g, w_in, ln_g, ln_b, w_mix, b_mix, w_out, *[w for w, _ in casts])


def _sgu_mix_tables(w_s, b_s, seq_len):
    length = min(seq_len, SGU_CHUNK)
    blk = np.arange(SGU_CHUNK) // RET_CHUNK
    w = jnp.where(blk[None, :] <= blk[:, None], w_s, 0)[:, :length, :length]
    reps = ROW_TILE // length
    seg = np.arange(ROW_TILE) // length
    tile = jnp.where(seg[None, :] == seg[:, None], jnp.tile(w, (1, reps, reps)), 0)
    bias = jnp.tile(b_s[:, :length].T, (reps, 1))
    return tile.astype(BF16), bias


def _ffn_body(s, t, final_norm, x_ref, mod_ref, g_ref, wup_ref, cw_ref, cb_ref,
              wdn_ref, buf_ref, fg_ref, xo_ref, co_ref, carry_ref):
    tm = s * t
    x = x_ref[...]
    mod = mod_ref[...]
    h = _modulated(x, g_ref[...], mod[:, 3 * D_MODEL:4 * D_MODEL],
                   mod[:, 4 * D_MODEL:5 * D_MODEL], s, t).astype(BF16)
    if s == 1:
        @pl.when(pl.program_id(1) == 0)
        def _():
            carry_ref[...] = buf_ref[0]

    def up(c, w):
        return [_dot(h, wup_ref[:, base + c:base + c + w]) for base in (0, FFN_D)]

    def conv(a, lo, w):
        r1 = pltpu.roll(a, 1, 0)
        r2 = pltpu.roll(a, 2, 0)
        if s == 1:
            p0 = carry_ref[0:1, lo:lo + w]
            p1 = carry_ref[1:2, lo:lo + w]
            carry_ref[:, lo:lo + w] = a[tm - 2:, :]
            row = lax.broadcasted_iota(jnp.int32, (8, 1), 0)
            a1 = jnp.concatenate([jnp.where(row == 0, p1, r1[:8]), r1[8:]], axis=0)
            a2 = jnp.concatenate(
                [jnp.where(row == 0, p0, jnp.where(row == 1, p1, r2[:8])), r2[8:]], axis=0)
        else:
            p0 = _rows(buf_ref[:, 0, lo:lo + w], s, t)
            p1 = _rows(buf_ref[:, 1, lo:lo + w], s, t)
            co_ref[:, :, lo:lo + w] = a.reshape(s, t, w)[:, t - 2:, :]
            pos = lax.broadcasted_iota(jnp.int32, (tm, 1), 0) & (t - 1)
            a1 = jnp.where(pos == 0, p1, r1)
            a2 = jnp.where(pos == 0, p0, jnp.where(pos == 1, p1, r2))
        c = cb_ref[:, lo:lo + w] + a2 * cw_ref[0:1, lo:lo + w]
        c = c + a1 * cw_ref[1:2, lo:lo + w]
        return c + a * cw_ref[2:3, lo:lo + w]

    y = jnp.zeros((tm, D_MODEL), F32)
    nxt = up(*FFN_CHUNKS[0])
    for k, (c, w) in enumerate(FFN_CHUNKS):
        cur = nxt
        if k + 1 < len(FFN_CHUNKS):
            nxt = up(*FFN_CHUNKS[k + 1])
        gate = conv(cur[0], c, w)
        val = conv(cur[1], FFN_D + c, w)
        hid = (_silu(gate) * val).astype(BF16)
        y = y + _dot(hid, wdn_ref[c:c + w, :D_MODEL])
    if s == 1:
        co_ref[0] = carry_ref[...]
    xn = x + _rows(mod[:, 5 * D_MODEL:6 * D_MODEL], s, t) * y
    if final_norm:
        xn = _rmsnorm(xn, fg_ref[...])
    xo_ref[...] = xn


def _ffn(x, mod, g, w_up, conv_w, conv_b, w_down, buf, final_g, s, t, final_norm, name,
         casts=()):
    nb, r, _ = x.shape
    tm = s * t
    assert t & (t - 1) == 0 and t >= CONV_W - 1
    grid = (nb, r // tm)
    in_specs = [
        pl.BlockSpec((None, tm, D_MODEL), lambda i, j: (i, j, 0)),
        pl.BlockSpec((None, s, N_MOD * D_MODEL), lambda i, j: (i, 0, 0)),
        _resident((1, D_MODEL)),
        _resident(w_up.shape),
        _resident((CONV_W, 2 * FFN_D)),
        _resident((1, 2 * FFN_D)),
        _resident(w_down.shape),
        pl.BlockSpec((None, s, CONV_W - 1, 2 * FFN_D), lambda i, j: (i, 0, 0, 0)),
        _resident((1, D_MODEL)),
    ]
    out_specs = [
        pl.BlockSpec((None, tm, D_MODEL), lambda i, j: (i, j, 0)),
        pl.BlockSpec((None, s, CONV_W - 1, 2 * FFN_D), lambda i, j: (i, 0, 0, 0)),
    ]
    out_shape = [
        jax.ShapeDtypeStruct(x.shape, F32),
        jax.ShapeDtypeStruct(buf.shape, F32),
    ]
    cast_in, cast_out, cast_shape = _cast_specs(casts, grid)
    return pl.pallas_call(
        _with_casts(functools.partial(_ffn_body, s, t, final_norm),
                    len(in_specs), len(out_specs), len(casts)),
        grid=grid,
        in_specs=in_specs + cast_in,
        out_specs=out_specs + cast_out,
        out_shape=out_shape + cast_shape,
        scratch_shapes=[pltpu.VMEM((CONV_W - 1, 2 * FFN_D), F32)],
        compiler_params=_params(56),
        name=name,
    )(x, mod, g, w_up, conv_w, conv_b, w_down, buf, final_g, *[w for w, _ in casts])


def kernel(x_prompt, x_sample, state_ret, state_ffn_conv, c_prompt, c_sample,
           w_ada, b_ada, norm_mix_g, norm_ffn_g,
           ret_w_in, ret_gn_g, ret_w_out,
           sgu_w_in, sgu_ln_g, sgu_ln_b, sgu_w_s, sgu_b_s, sgu_w_out,
           ffn_w_up, ffn_conv_w, ffn_conv_b, ffn_w_down, final_g):
    bp, seq, _ = x_prompt.shape
    bs, dec, _ = x_sample.shape
    depth = w_ada.shape[0]
    ss = ROW_TILE // dec
    nbs = bs // ss

    mod = _ada(jnp.concatenate([c_prompt, c_sample], axis=0), w_ada, b_ada)
    mod_p = mod[:, :bp].reshape(depth, bp, 1, -1)
    mod_s = mod[:, bp:].reshape(depth, nbs, ss, -1)

    xp = x_prompt
    xs = x_sample.reshape(nbs, ROW_TILE, D_MODEL)
    fg = final_g.reshape(1, D_MODEL)
    zero_buf = jnp.zeros((bp, 1, CONV_W - 1, 2 * FFN_D), F32)

    def mixer_weights(layer):
        src = (ret_w_in, ret_w_out) if layer % 2 == 0 else (sgu_w_in, sgu_w_out)
        return [(w, layer // 2) for w in src]

    w_in, w_out = [_to_resident_bf16(w[j]) for w, j in mixer_weights(0)]

    ret_p, ret_s, conv_p, conv_s, sgu_s = [], [], [], [], []
    for i in range(depth):
        j = i // 2
        g_mix = norm_mix_g[i].reshape(1, D_MODEL)
        ffn_casts = [(ffn_w_up, i), (ffn_w_down, i)]
        if i % 2 == 0:
            gn_g = ret_gn_g[j].reshape(1, RET_V)
            cos_p, sin_p = _rope_tables(0, seq)
            xp, sp, w_up, w_down = _ret_prompt(
                xp, mod_p[i], g_mix, w_in, cos_p, sin_p,
                *_decay_tables(ROW_TILE, RET_CHUNK), gn_g, w_out, casts=ffn_casts)
            cos_s, sin_s = _rope_tables(PAST_LEN, dec)
            proj = _proj_sample(xs, mod_s[i], g_mix, w_in, 2 * RET_QK + 2 * RET_V, ss, dec)
            gated, s_new = _ret_sample_core(
                proj.reshape(bs, dec, -1), state_ret[j], cos_s, sin_s,
                *_decay_tables(dec, min(dec, RET_CHUNK)), gn_g)
            xs = _out_sample(gated.reshape(nbs, ROW_TILE, RET_V), xs, mod_s[i],
                             w_out, ss, dec)
            ret_p.append(sp)
            ret_s.append(s_new)
        else:
            ln_g = sgu_ln_g[j].reshape(1, SGU_D)
            ln_b = sgu_ln_b[j].reshape(1, SGU_D)
            wm_p, bm_p = _sgu_mix_tables(sgu_w_s[j], sgu_b_s[j], seq)
            wm_s, bm_s = _sgu_mix_tables(sgu_w_s[j], sgu_b_s[j], dec)
            xp, w_up, w_down = _sgu(xp, mod_p[i], g_mix, w_in, ln_g, ln_b, wm_p, bm_p,
                                    w_out, 1, MIXER_ROW_TILE, False, casts=ffn_casts)
            xs, vs = _sgu(xs, mod_s[i], g_mix, w_in, ln_g, ln_b, wm_s, bm_s, w_out,
                          ss, dec, True)
            sgu_s.append(vs.reshape(bs, dec, SGU_D))
        g_ffn = norm_ffn_g[i].reshape(1, D_MODEL)
        conv_b = ffn_conv_b[i].reshape(1, 2 * FFN_D)
        last = i == depth - 1
        xp, cp, *next_mixer = _ffn(
            xp, mod_p[i], g_ffn, w_up, ffn_conv_w[i], conv_b, w_down, zero_buf, fg,
            1, FFN_ROW_TILE, last, "ffn_prompt_%d" % i,
            casts=[] if last else mixer_weights(i + 1))
        xs, cs = _ffn(xs, mod_s[i], g_ffn, w_up, ffn_conv_w[i], conv_b, w_down,
                      state_ffn_conv[i].reshape(nbs, ss, CONV_W - 1, 2 * FFN_D),
                      fg, ss, dec, last, "ffn_sample_%d" % i)
        if not last:
            w_in, w_out = next_mixer
        conv_p.append(cp.reshape(bp, CONV_W - 1, 2 * FFN_D))
        conv_s.append(cs.reshape(bs, CONV_W - 1, 2 * FFN_D))

    return (xp, xs.reshape(bs, dec, D_MODEL), jnp.stack(ret_p), jnp.stack(ret_s),
            jnp.stack(conv_p), jnp.stack(conv_s), jnp.stack(sgu_s))
```

```python
import functools

import jax
import jax.numpy as jnp
import numpy as np
from jax import lax
from jax.experimental import pallas as pl
from jax.experimental.pallas import tpu as pltpu

D_MODEL = 1024
N_MOD = 6
RET_HEADS = 4
RET_DK = 256
RET_DV = 512
RET_QK = RET_HEADS * RET_DK
RET_V = RET_HEADS * RET_DV
RET_CHUNK = 64
ROPE_BASE = 10000.0
PAST_LEN = 1024
SGU_CHUNK = 128
SGU_GROUPS = 4
SGU_D = 3 * D_MODEL
SGU_DG = SGU_D // SGU_GROUPS
FFN_D = 2816
CONV_W = 3
EPS = 1e-6

ROW_TILE = 256
FFN_ROW_TILE = 512
MIXER_ROW_TILE = 512
SAMPLE_ROW_TILE = 512
RET_SAMPLE_STREAMS = 4
FFN_CHUNKS = ((0, 1024), (1024, 1024), (2048, 768))
MIB = 1024 * 1024
LANES = 128
SUBLANES = 8
ROW_GROUP = SUBLANES * SUBLANES
LOG2_E = float(np.log2(np.e))
BF16_SUBLANES = 16

BF16 = jnp.bfloat16
F32 = jnp.float32


def _dot(a, b):
    return jnp.dot(a, b, preferred_element_type=F32)


def _resident(shape):
    nd = len(shape)
    return pl.BlockSpec(shape, lambda *_: (0,) * nd, pipeline_mode=pl.Buffered(1))


def _padded_cols(cols):
    return cols + LANES if cols % (8 * LANES) == 0 else cols


def _to_resident_bf16(w):
    rows, cols = w.shape
    return jnp.pad(w.astype(BF16), ((0, 0), (0, _padded_cols(cols) - cols)))


def _cast_specs(jobs, grid):
    steps = grid[0] * grid[1]
    in_specs, out_specs, out_shape = [], [], []
    for w, layer in jobs:
        _, rows, cols = w.shape
        parts = max(p for p in range(1, steps + 1)
                    if rows % p == 0 and (rows // p) % BF16_SUBLANES == 0)

        def slab(i, j, parts=parts):
            return jnp.minimum(i * grid[1] + j, parts - 1)

        in_specs.append(pl.BlockSpec((None, rows // parts, cols),
                                     lambda i, j, layer=layer, slab=slab: (layer, slab(i, j), 0)))
        out_specs.append(pl.BlockSpec((rows // parts, _padded_cols(cols)),
                                      lambda i, j, slab=slab: (slab(i, j), 0)))
        out_shape.append(jax.ShapeDtypeStruct((rows, _padded_cols(cols)), BF16))
    return in_specs, out_specs, out_shape


def _with_casts(body, n_in, n_out, n_cast):
    def wrapped(*refs):
        ins, refs = refs[:n_in], refs[n_in:]
        cast_in, refs = refs[:n_cast], refs[n_cast:]
        outs, refs = refs[:n_out], refs[n_out:]
        cast_out, scratch = refs[:n_cast], refs[n_cast:]
        for src, dst in zip(cast_in, cast_out):
            rows, cols = src.shape
            dst[:, :cols] = src[...].astype(BF16)
            if dst.shape[1] > cols:
                dst[:, cols:] = jnp.zeros((rows, dst.shape[1] - cols), BF16)
        body(*ins, *outs, *scratch)
    return wrapped


def _params(vmem_mib):
    return pltpu.CompilerParams(
        dimension_semantics=("arbitrary", "arbitrary"),
        vmem_limit_bytes=vmem_mib * MIB)


def _rows(v, s, t):
    if s == 1:
        return v
    return jnp.broadcast_to(v[:, None, :], (s, t, v.shape[-1])).reshape(s * t, v.shape[-1])


def _rmsnorm(x, g):
    ms = jnp.mean(x * x, axis=-1, keepdims=True)
    return x * lax.rsqrt(ms + EPS) * g


def _modulated(x, g, shift, scale, s, t):
    return _rmsnorm(x, g) * (1.0 + _rows(scale, s, t)) + _rows(shift, s, t)


def _silu(x):
    return x / (1.0 + jnp.exp2(x * -LOG2_E))


def _gelu(x):
    k = -2.0 * (2.0 / np.pi) ** 0.5 * LOG2_E
    return x / (1.0 + jnp.exp2(x * (k + (k * 0.044715) * (x * x))))


def _layernorm_nogain(o):
    mu = jnp.mean(o, axis=-1, keepdims=True)
    d = o - mu
    var = jnp.mean(d * d, axis=-1, keepdims=True)
    return d * lax.rsqrt(var + EPS)


def _swap_rows(val, scr):
    rows, cols = val.shape
    per = ROW_GROUP // SUBLANES
    for lane in range(cols // LANES):
        for k in range(rows // SUBLANES):
            grp, v = divmod(k, per)
            scr.at[lane][pl.ds(ROW_GROUP * grp + v, SUBLANES, stride=per), :] = (
                val[SUBLANES * k:SUBLANES * (k + 1), LANES * lane:LANES * (lane + 1)])
    return jnp.concatenate([scr[lane] for lane in range(cols // LANES)], axis=-1)


def _swap_index(n):
    per = ROW_GROUP // SUBLANES
    return np.arange(n).reshape(n // ROW_GROUP, SUBLANES, per).transpose(0, 2, 1).reshape(n)


def _ada_body(c_ref, w_ref, b_ref, o_ref):
    a = _silu(c_ref[...]).astype(BF16)
    o_ref[...] = _dot(a, w_ref[...].astype(BF16)) + b_ref[...]


def _ada(c_all, w_ada, b_ada):
    depth = w_ada.shape[0]
    n = c_all.shape[0]
    return pl.pallas_call(
        _ada_body,
        grid=(depth, N_MOD),
        in_specs=[
            pl.BlockSpec((n, D_MODEL), lambda i, j: (0, 0)),
            pl.BlockSpec((None, D_MODEL, D_MODEL), lambda i, j: (i, 0, j)),
            pl.BlockSpec((None, 1, D_MODEL), lambda i, j: (i, 0, j)),
        ],
        out_specs=pl.BlockSpec((None, n, D_MODEL), lambda i, j: (i, 0, j)),
        out_shape=jax.ShapeDtypeStruct((depth, n, N_MOD * D_MODEL), F32),
        compiler_params=_params(32),
        name="ada",
    )(c_all, w_ada, b_ada.reshape(depth, 1, N_MOD * D_MODEL))


def _rotary(x, cos, sin):
    half = x.shape[-1] // 2
    x1, x2 = x[:, :half], x[:, half:]
    return jnp.concatenate([x1 * cos - x2 * sin, x1 * sin + x2 * cos], axis=-1)


def _ret_head(q, k, v, s_prev, cos, sin, mask, qdec, kdec, gs):
    qb = (_rotary(q, cos, sin) * (RET_DK ** -0.5)).astype(BF16)
    kr = _rotary(k, cos, sin)
    kb = kr.astype(BF16)
    vb = v.astype(BF16)
    scores = lax.dot_general(qb, kb, (((1,), (1,)), ((), ())),
                             preferred_element_type=F32) * mask
    o = _dot(scores.astype(BF16), vb) + qdec * _dot(qb, s_prev.astype(BF16))
    kd = (kr * kdec).T.astype(BF16)
    s_new = gs * s_prev + _dot(kd, vb)
    return o, s_new


def _ret_prompt_body(x_ref, mod_ref, g_ref, win_ref, cos_ref, sin_ref, mask_ref,
                     qdec_ref, kdec_ref, gs_ref, gng_ref, wout_ref,
                     xo_ref, so_ref, s_ref, *maybe_swap_ref):
    t = pl.program_id(1)

    @pl.when(t == 0)
    def _():
        s_ref[...] = jnp.zeros_like(s_ref)

    x = x_ref[...]
    if maybe_swap_ref:
        x = _swap_rows(x, maybe_swap_ref[0])
    mod = mod_ref[...]
    tm = x.shape[0]
    h = _modulated(x, g_ref[...], mod[:, 0:D_MODEL], mod[:, D_MODEL:2 * D_MODEL],
                   1, tm).astype(BF16)
    y = jnp.zeros((tm, D_MODEL), F32)
    for hd in range(RET_HEADS):
        q = _dot(h, win_ref[:, hd * RET_DK:(hd + 1) * RET_DK])
        k = _dot(h, win_ref[:, RET_QK + hd * RET_DK:RET_QK + (hd + 1) * RET_DK])
        v = _dot(h, win_ref[:, 2 * RET_QK + hd * RET_DV:2 * RET_QK + (hd + 1) * RET_DV])
        gate = _dot(h, win_ref[:, 2 * RET_QK + RET_V + hd * RET_DV:
                               2 * RET_QK + RET_V + (hd + 1) * RET_DV])
        state = s_ref[hd]
        blocks = []
        for r in range(0, tm, ROW_TILE):
            rows = slice(r, r + ROW_TILE)
            o, state = _ret_head(q[rows], k[rows], v[rows], state,
                                 cos_ref[rows, :], sin_ref[rows, :], mask_ref[hd],
                                 qdec_ref[:, hd:hd + 1], kdec_ref[:, hd:hd + 1],
                                 gs_ref[:, hd:hd + 1])
            blocks.append(o)
        s_ref[hd] = state
        o = blocks[0] if len(blocks) == 1 else jnp.concatenate(blocks, axis=0)
        on = _layernorm_nogain(o) * gng_ref[:, hd * RET_DV:(hd + 1) * RET_DV]
        gated = (_silu(gate) * on).astype(BF16)
        y = y + _dot(gated, wout_ref[hd * RET_DV:(hd + 1) * RET_DV, :D_MODEL])
    xo_ref[...] = x + mod[:, 2 * D_MODEL:3 * D_MODEL] * y

    @pl.when(t == pl.num_programs(1) - 1)
    def _():
        so_ref[...] = s_ref[...]


def _ret_prompt(x, mod, g, w_in, cos, sin, mask, qdec, kdec, gs, gn_g, w_out, swap_in,
                casts=()):
    b, seq, _ = x.shape
    tm = MIXER_ROW_TILE
    grid = (b, seq // tm)
    in_specs = [
        pl.BlockSpec((None, tm, D_MODEL), lambda i, t: (i, t, 0)),
        pl.BlockSpec((None, 1, N_MOD * D_MODEL), lambda i, t: (i, 0, 0)),
        _resident((1, D_MODEL)),
        _resident(w_in.shape),
        pl.BlockSpec((tm, RET_DK // 2), lambda i, t: (t, 0)),
        pl.BlockSpec((tm, RET_DK // 2), lambda i, t: (t, 0)),
        _resident(mask.shape),
        _resident(qdec.shape),
        _resident(kdec.shape),
        _resident(gs.shape),
        _resident((1, RET_V)),
        _resident(w_out.shape),
    ]
    out_specs = [
        pl.BlockSpec((None, tm, D_MODEL), lambda i, t: (i, t, 0)),
        pl.BlockSpec((None, RET_HEADS, RET_DK, RET_DV), lambda i, t: (i, 0, 0, 0)),
    ]
    out_shape = [
        jax.ShapeDtypeStruct(x.shape, F32),
        jax.ShapeDtypeStruct((b, RET_HEADS, RET_DK, RET_DV), F32),
    ]
    cast_in, cast_out, cast_shape = _cast_specs(casts, grid)
    return pl.pallas_call(
        _with_casts(_ret_prompt_body, len(in_specs), len(out_specs), len(casts)),
        grid=grid,
        in_specs=in_specs + cast_in,
        out_specs=out_specs + cast_out,
        out_shape=out_shape + cast_shape,
        scratch_shapes=[pltpu.VMEM((RET_HEADS, RET_DK, RET_DV), F32)] + (
            [pltpu.VMEM((D_MODEL // LANES, tm, LANES), F32)] if swap_in else []),
        compiler_params=_params(56),
        name="ret_prompt",
    )(x, mod.reshape(b, 1, -1), g, w_in, cos, sin, mask, qdec, kdec, gs, gn_g, w_out,
      *[w for w, _ in casts])


def _proj_body(s, t, x_ref, mod_ref, g_ref, w_ref, o_ref):
    mod = mod_ref[...]
    h = _modulated(x_ref[...], g_ref[...], mod[:, 0:D_MODEL],
                   mod[:, D_MODEL:2 * D_MODEL], s, t).astype(BF16)
    o_ref[...] = _dot(h, w_ref[:, :o_ref.shape[-1]])


def _proj_sample(x, mod, g, w, n, s, t):
    nb, tm, _ = x.shape
    return pl.pallas_call(
        functools.partial(_proj_body, s, t),
        grid=(nb, 1),
        in_specs=[
            pl.BlockSpec((None, tm, D_MODEL), lambda i, j: (i, 0, 0)),
            pl.BlockSpec((None, s, N_MOD * D_MODEL), lambda i, j: (i, 0, 0)),
            _resident((1, D_MODEL)),
            _resident(w.shape),
        ],
        out_specs=pl.BlockSpec((None, tm, n), lambda i, j: (i, 0, 0)),
        out_shape=jax.ShapeDtypeStruct((nb, tm, n), F32),
        compiler_params=_params(48),
        name="ret_sample_proj",
    )(x, mod, g, w)


def _ret_sample_core_body(p_ref, s0_ref, cos_ref, sin_ref, mask_ref, qdec_ref,
                          kdec_ref, gs_ref, gng_ref, o_ref, so_ref):
    cos = cos_ref[...]
    sin = sin_ref[...]
    for b in range(RET_SAMPLE_STREAMS):
        for hd in range(RET_HEADS):
            q = p_ref[b, :, hd * RET_DK:(hd + 1) * RET_DK]
            k = p_ref[b, :, RET_QK + hd * RET_DK:RET_QK + (hd + 1) * RET_DK]
            v = p_ref[b, :, 2 * RET_QK + hd * RET_DV:2 * RET_QK + (hd + 1) * RET_DV]
            gate = p_ref[b, :, 2 * RET_QK + RET_V + hd * RET_DV:
                         2 * RET_QK + RET_V + (hd + 1) * RET_DV]
            o, s_new = _ret_head(q, k, v, s0_ref[b, hd], cos, sin, mask_ref[hd],
                                 qdec_ref[:, hd:hd + 1], kdec_ref[:, hd:hd + 1],
                                 gs_ref[:, hd:hd + 1])
            so_ref[b, hd] = s_new
            on = _layernorm_nogain(o) * gng_ref[:, hd * RET_DV:(hd + 1) * RET_DV]
            o_ref[b, :, hd * RET_DV:(hd + 1) * RET_DV] = _silu(gate) * on


def _ret_sample_core(proj, s0, cos, sin, mask, qdec, kdec, gs, gn_g):
    b, t, n = proj.shape
    nb = RET_SAMPLE_STREAMS
    return pl.pallas_call(
        _ret_sample_core_body,
        grid=(b // nb, 1),
        in_specs=[
            pl.BlockSpec((nb, t, n), lambda i, j: (i, 0, 0)),
            pl.BlockSpec((nb, RET_HEADS, RET_DK, RET_DV), lambda i, j: (i, 0, 0, 0)),
            _resident(cos.shape),
            _resident(sin.shape),
            _resident(mask.shape),
            _resident(qdec.shape),
            _resident(kdec.shape),
            _resident(gs.shape),
            _resident((1, RET_V)),
        ],
        out_specs=[
            pl.BlockSpec((nb, t, RET_V), lambda i, j: (i, 0, 0)),
            pl.BlockSpec((nb, RET_HEADS, RET_DK, RET_DV), lambda i, j: (i, 0, 0, 0)),
        ],
        out_shape=[
            jax.ShapeDtypeStruct((b, t, RET_V), F32),
            jax.ShapeDtypeStruct(s0.shape, F32),
        ],
        compiler_params=_params(48),
        name="ret_sample_core",
    )(proj, s0, cos, sin, mask, qdec, kdec, gs, gn_g)


def _out_body(s, t, a_ref, x_ref, mod_ref, w_ref, o_ref):
    y = _dot(a_ref[...].astype(BF16), w_ref[:, :D_MODEL])
    gate = _rows(mod_ref[...][:, 2 * D_MODEL:3 * D_MODEL], s, t)
    o_ref[...] = x_ref[...] + gate * y


def _out_sample(a, x, mod, w, s, t):
    nb, tm, k = a.shape
    return pl.pallas_call(
        functools.partial(_out_body, s, t),
        grid=(nb, 1),
        in_specs=[
            pl.BlockSpec((None, tm, k), lambda i, j: (i, 0, 0)),
            pl.BlockSpec((None, tm, D_MODEL), lambda i, j: (i, 0, 0)),
            pl.BlockSpec((None, s, N_MOD * D_MODEL), lambda i, j: (i, 0, 0)),
            _resident(w.shape),
        ],
        out_specs=pl.BlockSpec((None, tm, D_MODEL), lambda i, j: (i, 0, 0)),
        out_shape=jax.ShapeDtypeStruct(x.shape, F32),
        compiler_params=_params(32),
        name="ret_sample_out",
    )(a, x, mod, w)


def _decay_tables(block, chunk):
    log_gamma = np.log(1.0 - np.exp2(-5.0 - np.arange(RET_HEADS)))
    idx = np.arange(block, dtype=np.float64)
    ch = np.arange(block) // chunk
    dist = np.abs(idx[:, None] - idx[None, :])
    decay = np.exp(log_gamma[:, None, None] * dist[None])
    mask = np.where((ch[None, :] <= ch[:, None])[None], decay, 0.0)
    qdec = np.exp((idx[:, None] + 1.0) * log_gamma[None, :])
    kdec = np.exp((block - 1.0 - idx)[:, None] * log_gamma[None, :])
    gs = np.exp(block * log_gamma)[None, :]
    return tuple(jnp.asarray(a, F32) for a in (mask, qdec, kdec, gs))


def _rope_tables(first, count):
    half = RET_DK // 2
    inv = np.power(ROPE_BASE, -np.arange(half, dtype=np.float64) / half)
    ang = (first + np.arange(count, dtype=np.float64))[:, None] * inv[None, :]
    return jnp.asarray(np.cos(ang), F32), jnp.asarray(np.sin(ang), F32)


def _sgu_body(s, t, emit_v, x_ref, mod_ref, g_ref, win_ref, lng_ref, lnb_ref,
              wmix_ref, bmix_ref, wout_ref, xo_ref, *maybe_v_ref):
    x = x_ref[...]
    mod = mod_ref[...]
    h = _modulated(x, g_ref[...], mod[:, 0:D_MODEL], mod[:, D_MODEL:2 * D_MODEL],
                   s, t).astype(BF16)
    groups = range(SGU_GROUPS)

    def cols(grp, base):
        return slice(base + grp * SGU_DG, base + (grp + 1) * SGU_DG)

    def proj(grp, base):
        return _dot(h, win_ref[:, cols(grp, base)])

    pv = [proj(0, SGU_D)]
    v = []
    for grp in groups:
        pv.append(proj(grp + 1, SGU_D) if grp + 1 < SGU_GROUPS else proj(0, 0))
        v.append(_gelu(pv[grp]))
    pu = [pv.pop(), proj(1, 0)]
    mu = sum(jnp.sum(vg, axis=-1, keepdims=True) for vg in v) * (1.0 / SGU_D)
    d = [vg - mu for vg in v]
    var = sum(jnp.sum(dg * dg, axis=-1, keepdims=True) for dg in d) * (1.0 / SGU_D)
    rs = lax.rsqrt(var + EPS)
    vb = []
    for grp in groups:
        vn = d[grp] * rs * lng_ref[:, cols(grp, 0)] + lnb_ref[:, cols(grp, 0)]
        if emit_v:
            maybe_v_ref[0][:, cols(grp, 0)] = vn
        vb.append(vn.astype(BF16))
    def mix(grp):
        parts = [_dot(wmix_ref[grp], vb[grp][r:r + ROW_TILE]) + bmix_ref[:, grp:grp + 1]
                 for r in range(0, s * t, ROW_TILE)]
        return parts[0] if len(parts) == 1 else jnp.concatenate(parts, axis=0)

    y = jnp.zeros((s * t, D_MODEL), F32)
    mixed = mix(0)
    for grp in groups:
        if grp + 2 < SGU_GROUPS:
            pu.append(proj(grp + 2, 0))
        u = _gelu(pu[grp])
        cur = mixed
        if grp + 1 < SGU_GROUPS:
            mixed = mix(grp + 1)
        gated = (u * cur).astype(BF16)
        y = y + _dot(gated, wout_ref[cols(grp, 0), :D_MODEL])
    xo_ref[...] = x + _rows(mod[:, 2 * D_MODEL:3 * D_MODEL], s, t) * y


def _sgu(x, mod, g, w_in, ln_g, ln_b, w_mix, b_mix, w_out, s, t, emit_v, casts=()):
    nb, r, _ = x.shape
    tm = s * t
    grid = (nb, r // tm)
    in_specs = [
        pl.BlockSpec((None, tm, D_MODEL), lambda i, j: (i, j, 0)),
        pl.BlockSpec((None, s, N_MOD * D_MODEL), lambda i, j: (i, 0, 0)),
        _resident((1, D_MODEL)),
        _resident(w_in.shape),
        _resident((1, SGU_D)),
        _resident((1, SGU_D)),
        _resident(w_mix.shape),
        _resident(b_mix.shape),
        _resident(w_out.shape),
    ]
    out_specs = [pl.BlockSpec((None, tm, D_MODEL), lambda i, j: (i, j, 0))]
    out_shape = [jax.ShapeDtypeStruct(x.shape, F32)]
    if emit_v:
        out_specs.append(pl.BlockSpec((None, tm, SGU_D), lambda i, j: (i, j, 0)))
        out_shape.append(jax.ShapeDtypeStruct((nb, r, SGU_D), F32))
    cast_in, cast_out, cast_shape = _cast_specs(casts, grid)
    return pl.pallas_call(
        _with_casts(functools.partial(_sgu_body, s, t, emit_v),
                    len(in_specs), len(out_specs), len(casts)),
        grid=grid,
        in_specs=in_specs + cast_in,
        out_specs=out_specs + cast_out,
        out_shape=out_shape + cast_shape,
        compiler_params=_params(56),
        name="sgu_sample" if emit_v else "sgu_prompt",
    )(x, mod, g, w_in, ln_g, ln_b, w_mix, b_mix, w_out, *[w for w, _ in casts])


def _sgu_mix_tables(w_s, b_s, seq_len):
    length = min(seq_len, SGU_CHUNK)
    blk = np.arange(SGU_CHUNK) // RET_CHUNK
    w = jnp.where(blk[None, :] <= blk[:, None], w_s, 0)[:, :length, :length]
    reps = ROW_TILE // length
    seg = np.arange(ROW_TILE) // length
    tiled = jnp.broadcast_to(w[:, None, :, None, :], (SGU_GROUPS, reps, length, reps, length))
    tile = jnp.where(seg[None, :] == seg[:, None], tiled.reshape(SGU_GROUPS, ROW_TILE, ROW_TILE), 0)
    bias = jnp.tile(b_s[:, :length].T, (reps, 1))
    return tile.astype(BF16), bias


def _ffn_body(s, t, final_norm, swap_out, x_ref, mod_ref, g_ref, wup_ref, cw_ref, cb_ref,
              wdn_ref, buf_ref, fg_ref, xo_ref, co_ref, *scratch):
    tm = s * t
    x = x_ref[...]
    mod = mod_ref[...]
    h = _modulated(x, g_ref[...], mod[:, 3 * D_MODEL:4 * D_MODEL],
                   mod[:, 4 * D_MODEL:5 * D_MODEL], s, t).astype(BF16)
    if s == 1:
        carry_ref = scratch[0]

        @pl.when(pl.program_id(1) == 0)
        def _():
            carry_ref[...] = buf_ref[0]

    def up(c, w):
        return [_dot(h, wup_ref[:, base + c:base + c + w]) for base in (0, FFN_D)]

    def shifted_swapped(a, lo, w):
        groups = tm // ROW_GROUP
        per = ROW_GROUP // SUBLANES
        a4 = a.reshape(groups, per, SUBLANES, w)
        p0 = carry_ref[0:1, lo:lo + w]
        p1 = carry_ref[1:2, lo:lo + w]
        carry_ref[0:1, lo:lo + w] = a4[groups - 1, per - 2, SUBLANES - 1:, :]
        carry_ref[1:2, lo:lo + w] = a4[groups - 1, per - 1, SUBLANES - 1:, :]
        sub = lax.broadcasted_iota(jnp.int32, (1, SUBLANES, 1), 1)

        def wrapped(col, before):
            r = pltpu.roll(col, 1, 1)
            prev = jnp.concatenate(
                [jnp.broadcast_to(before[None], (1, SUBLANES, w)), r[:-1]], axis=0)
            return jnp.where(sub == 0, prev, r)

        w1 = wrapped(a4[:, per - 1], p1)[:, None]
        w2 = wrapped(a4[:, per - 2], p0)[:, None]
        a1 = jnp.concatenate([w1, a4[:, :per - 1]], axis=1).reshape(tm, w)
        a2 = jnp.concatenate([w2, w1, a4[:, :per - 2]], axis=1).reshape(tm, w)
        return a1, a2

    def shifted_streams(a, lo, w):
        r1 = pltpu.roll(a, 1, 0)
        r2 = pltpu.roll(a, 2, 0)
        p0 = _rows(buf_ref[:, 0, lo:lo + w], s, t)
        p1 = _rows(buf_ref[:, 1, lo:lo + w], s, t)
        co_ref[:, :, lo:lo + w] = a.reshape(s, t, w)[:, t - 2:, :]
        pos = lax.broadcasted_iota(jnp.int32, (tm, 1), 0) & (t - 1)
        a1 = jnp.where(pos == 0, p1, r1)
        a2 = jnp.where(pos == 0, p0, jnp.where(pos == 1, p1, r2))
        return a1, a2

    def conv(a, lo, w):
        a1, a2 = (shifted_swapped if s == 1 else shifted_streams)(a, lo, w)
        c = cb_ref[:, lo:lo + w] + a2 * cw_ref[0:1, lo:lo + w]
        c = c + a1 * cw_ref[1:2, lo:lo + w]
        return c + a * cw_ref[2:3, lo:lo + w]

    y = jnp.zeros((tm, D_MODEL), F32)
    nxt = up(*FFN_CHUNKS[0])
    for k, (c, w) in enumerate(FFN_CHUNKS):
        cur = nxt
        if k + 1 < len(FFN_CHUNKS):
            nxt = up(*FFN_CHUNKS[k + 1])
        gate = conv(cur[0], c, w)
        val = conv(cur[1], FFN_D + c, w)
        hid = (_silu(gate) * val).astype(BF16)
        y = y + _dot(hid, wdn_ref[c:c + w, :D_MODEL])
    if s == 1:
        co_ref[0] = carry_ref[...]
    xn = x + _rows(mod[:, 5 * D_MODEL:6 * D_MODEL], s, t) * y
    if final_norm:
        xn = _rmsnorm(xn, fg_ref[...])
    if swap_out:
        xn = _swap_rows(xn, scratch[1])
    xo_ref[...] = xn


def _ffn(x, mod, g, w_up, conv_w, conv_b, w_down, buf, final_g, s, t, final_norm, swap_out,
         name, casts=()):
    nb, r, _ = x.shape
    tm = s * t
    assert t & (t - 1) == 0 and t >= CONV_W - 1 and (s > 1 or t % ROW_GROUP == 0)
    grid = (nb, r // tm)
    in_specs = [
        pl.BlockSpec((None, tm, D_MODEL), lambda i, j: (i, j, 0)),
        pl.BlockSpec((None, s, N_MOD * D_MODEL), lambda i, j: (i, 0, 0)),
        _resident((1, D_MODEL)),
        _resident(w_up.shape),
        _resident((CONV_W, 2 * FFN_D)),
        _resident((1, 2 * FFN_D)),
        _resident(w_down.shape),
        pl.BlockSpec((None, s, CONV_W - 1, 2 * FFN_D), lambda i, j: (i, 0, 0, 0)),
        _resident((1, D_MODEL)),
    ]
    out_specs = [
        pl.BlockSpec((None, tm, D_MODEL), lambda i, j: (i, j, 0)),
        pl.BlockSpec((None, s, CONV_W - 1, 2 * FFN_D), lambda i, j: (i, 0, 0, 0)),
    ]
    out_shape = [
        jax.ShapeDtypeStruct(x.shape, F32),
        jax.ShapeDtypeStruct(buf.shape, F32),
    ]
    cast_in, cast_out, cast_shape = _cast_specs(casts, grid)
    return pl.pallas_call(
        _with_casts(functools.partial(_ffn_body, s, t, final_norm, swap_out),
                    len(in_specs), len(out_specs), len(casts)),
        grid=grid,
        in_specs=in_specs + cast_in,
        out_specs=out_specs + cast_out,
        out_shape=out_shape + cast_shape,
        scratch_shapes=[] if s > 1 else [pltpu.VMEM((CONV_W - 1, 2 * FFN_D), F32)] + (
            [pltpu.VMEM((D_MODEL // LANES, tm, LANES), F32)] if swap_out else []),
        compiler_params=_params(56),
        name=name,
    )(x, mod, g, w_up, conv_w, conv_b, w_down, buf, final_g, *[w for w, _ in casts])


def kernel(x_prompt, x_sample, state_ret, state_ffn_conv, c_prompt, c_sample,
           w_ada, b_ada, norm_mix_g, norm_ffn_g,
           ret_w_in, ret_gn_g, ret_w_out,
           sgu_w_in, sgu_ln_g, sgu_ln_b, sgu_w_s, sgu_b_s, sgu_w_out,
           ffn_w_up, ffn_conv_w, ffn_conv_b, ffn_w_down, final_g):
    bp, seq, _ = x_prompt.shape
    bs, dec, _ = x_sample.shape
    depth = w_ada.shape[0]
    ss = SAMPLE_ROW_TILE // dec
    nbs = bs // ss

    mod = _ada(jnp.concatenate([c_prompt, c_sample], axis=0), w_ada, b_ada)
    mod_p = mod[:, :bp].reshape(depth, bp, 1, -1)
    mod_s = mod[:, bp:].reshape(depth, nbs, ss, -1)

    xp = x_prompt
    xs = x_sample.reshape(nbs, SAMPLE_ROW_TILE, D_MODEL)
    fg = final_g.reshape(1, D_MODEL)
    zero_buf = jnp.zeros((bp, 1, CONV_W - 1, 2 * FFN_D), F32)
    swap_seq = _swap_index(seq)
    swap_tile = _swap_index(ROW_TILE)

    def mixer_weights(layer):
        src = (ret_w_in, ret_w_out) if layer % 2 == 0 else (sgu_w_in, sgu_w_out)
        return [(w, layer // 2) for w in src]

    w_in, w_out = [_to_resident_bf16(w[j]) for w, j in mixer_weights(0)]

    ret_p, ret_s, conv_p, conv_s, sgu_s = [], [], [], [], []
    for i in range(depth):
        j = i // 2
        g_mix = norm_mix_g[i].reshape(1, D_MODEL)
        ffn_casts = [(ffn_w_up, i), (ffn_w_down, i)]
        if i % 2 == 0:
            gn_g = ret_gn_g[j].reshape(1, RET_V)
            cos_p, sin_p = [tab[swap_seq] for tab in _rope_tables(0, seq)]
            mask, qdec, kdec, gs = _decay_tables(ROW_TILE, RET_CHUNK)
            xp, sp, w_up, w_down = _ret_prompt(
                xp, mod_p[i], g_mix, w_in, cos_p, sin_p,
                mask[:, swap_tile][:, :, swap_tile], qdec[swap_tile], kdec[swap_tile], gs,
                gn_g, w_out, i == 0, casts=ffn_casts)
            cos_s, sin_s = _rope_tables(PAST_LEN, dec)
            proj = _proj_sample(xs, mod_s[i], g_mix, w_in, 2 * RET_QK + 2 * RET_V, ss, dec)
            gated, s_new = _ret_sample_core(
                proj.reshape(bs, dec, -1), state_ret[j], cos_s, sin_s,
                *_decay_tables(dec, min(dec, RET_CHUNK)), gn_g)
            xs = _out_sample(gated.reshape(nbs, SAMPLE_ROW_TILE, RET_V), xs, mod_s[i],
                             w_out, ss, dec)
            ret_p.append(sp)
            ret_s.append(s_new)
        else:
            ln_g = sgu_ln_g[j].reshape(1, SGU_D)
            ln_b = sgu_ln_b[j].reshape(1, SGU_D)
            wm_p, bm_p = _sgu_mix_tables(sgu_w_s[j], sgu_b_s[j], seq)
            wm_s, bm_s = _sgu_mix_tables(sgu_w_s[j], sgu_b_s[j], dec)
            xp, w_up, w_down = _sgu(
                xp, mod_p[i], g_mix, w_in, ln_g, ln_b,
                wm_p[:, swap_tile][:, :, swap_tile], bm_p[swap_tile], w_out,
                1, MIXER_ROW_TILE, False, casts=ffn_casts)
            xs, vs = _sgu(xs, mod_s[i], g_mix, w_in, ln_g, ln_b, wm_s, bm_s, w_out,
                          ss, dec, True)
            sgu_s.append(vs.reshape(bs, dec, SGU_D))
        g_ffn = norm_ffn_g[i].reshape(1, D_MODEL)
        conv_b = ffn_conv_b[i].reshape(1, 2 * FFN_D)
        last = i == depth - 1
        xp, cp, *next_mixer = _ffn(
            xp, mod_p[i], g_ffn, w_up, ffn_conv_w[i], conv_b, w_down, zero_buf, fg,
            1, FFN_ROW_TILE, last, last, "ffn_prompt_%d" % i,
            casts=[] if last else mixer_weights(i + 1))
        xs, cs = _ffn(xs, mod_s[i], g_ffn, w_up, ffn_conv_w[i], conv_b, w_down,
                      state_ffn_conv[i].reshape(nbs, ss, CONV_W - 1, 2 * FFN_D),
                      fg, ss, dec, last, False, "ffn_sample_%d" % i)
        if not last:
            w_in, w_out = next_mixer
        conv_p.append(cp.reshape(bp, CONV_W - 1, 2 * FFN_D))
        conv_s.append(cs.reshape(bs, CONV_W - 1, 2 * FFN_D))

    return (xp, xs.reshape(bs, dec, D_MODEL), jnp.stack(ret_p), jnp.stack(ret_s),
            jnp.stack(conv_p), jnp.stack(conv_s), jnp.stack(sgu_s))
```

```python
import functools

import jax
import jax.numpy as jnp
import numpy as np
from jax import lax
from jax.experimental import pallas as pl
from jax.experimental.pallas import tpu as pltpu

D_MODEL = 1024
N_MOD = 6
RET_HEADS = 4
RET_DK = 256
RET_DV = 512
RET_QK = RET_HEADS * RET_DK
RET_V = RET_HEADS * RET_DV
RET_CHUNK = 64
ROPE_BASE = 10000.0
PAST_LEN = 1024
SGU_CHUNK = 128
SGU_GROUPS = 4
SGU_D = 3 * D_MODEL
SGU_DG = SGU_D // SGU_GROUPS
FFN_D = 2816
CONV_W = 3
EPS = 1e-6

ROW_TILE = 256
FFN_ROW_TILE = 512
MIXER_ROW_TILE = 512
SAMPLE_ROW_TILE = 512
RET_SAMPLE_STREAMS = 4
FFN_CHUNKS = ((0, 1024), (1024, 1024), (2048, 768))
MIB = 1024 * 1024
FUSED_CALL_VMEM_MIB = 56
SMALL_CALL_VMEM_MIB = 48
ADA_CALL_VMEM_MIB = 32
LANES = 128
SUBLANES = 8
ROW_GROUP = SUBLANES * SUBLANES
LOG2_E = float(np.log2(np.e))
BF16_SUBLANES = 16

BF16 = jnp.bfloat16
F32 = jnp.float32


def _dot(a, b):
    return jnp.dot(a, b, preferred_element_type=F32)


def _resident(shape):
    nd = len(shape)
    return pl.BlockSpec(shape, lambda *_: (0,) * nd, pipeline_mode=pl.Buffered(1))


def _padded_cols(cols):
    return cols + LANES if cols % (8 * LANES) == 0 else cols


def _to_resident_bf16(w):
    rows, cols = w.shape
    return jnp.pad(w.astype(BF16), ((0, 0), (0, _padded_cols(cols) - cols)))


def _cast_specs(jobs, grid):
    steps = grid[0] * grid[1]
    in_specs, out_specs, out_shape = [], [], []
    for w, layer in jobs:
        _, rows, cols = w.shape
        parts = max(p for p in range(1, steps + 1)
                    if rows % p == 0 and (rows // p) % BF16_SUBLANES == 0)

        def slab(i, j, parts=parts):
            return jnp.minimum(i * grid[1] + j, parts - 1)

        in_specs.append(pl.BlockSpec((None, rows // parts, cols),
                                     lambda i, j, layer=layer, slab=slab: (layer, slab(i, j), 0)))
        out_specs.append(pl.BlockSpec((rows // parts, _padded_cols(cols)),
                                      lambda i, j, slab=slab: (slab(i, j), 0)))
        out_shape.append(jax.ShapeDtypeStruct((rows, _padded_cols(cols)), BF16))
    return in_specs, out_specs, out_shape


def _with_casts(body, n_in, n_out, n_cast):
    def wrapped(*refs):
        ins, refs = refs[:n_in], refs[n_in:]
        cast_in, refs = refs[:n_cast], refs[n_cast:]
        outs, refs = refs[:n_out], refs[n_out:]
        cast_out, scratch = refs[:n_cast], refs[n_cast:]
        for src, dst in zip(cast_in, cast_out):
            rows, cols = src.shape
            dst[:, :cols] = src[...].astype(BF16)
            if dst.shape[1] > cols:
                dst[:, cols:] = jnp.zeros((rows, dst.shape[1] - cols), BF16)
        body(*ins, *outs, *scratch)
    return wrapped


def _params(vmem_mib):
    return pltpu.CompilerParams(
        dimension_semantics=("arbitrary", "arbitrary"),
        vmem_limit_bytes=vmem_mib * MIB)


def _rows(v, s, t):
    if s == 1:
        return v
    return jnp.broadcast_to(v[:, None, :], (s, t, v.shape[-1])).reshape(s * t, v.shape[-1])


def _rmsnorm(x, g):
    ms = jnp.mean(x * x, axis=-1, keepdims=True)
    return x * lax.rsqrt(ms + EPS) * g


def _modulated(x, g, shift, scale, s, t):
    return _rmsnorm(x, g) * (1.0 + _rows(scale, s, t)) + _rows(shift, s, t)


def _silu(x):
    return x / (1.0 + jnp.exp2(x * -LOG2_E))


def _gelu(x):
    k = -2.0 * (2.0 / np.pi) ** 0.5 * LOG2_E
    return x / (1.0 + jnp.exp2(x * (k + (k * 0.044715) * (x * x))))


def _layernorm_nogain(o):
    mu = jnp.mean(o, axis=-1, keepdims=True)
    d = o - mu
    var = jnp.mean(d * d, axis=-1, keepdims=True)
    return d * lax.rsqrt(var + EPS)


def _swap_rows(val, scr):
    rows, cols = val.shape
    per = ROW_GROUP // SUBLANES
    for lane in range(cols // LANES):
        for k in range(rows // SUBLANES):
            grp, v = divmod(k, per)
            scr.at[lane][pl.ds(ROW_GROUP * grp + v, SUBLANES, stride=per), :] = (
                val[SUBLANES * k:SUBLANES * (k + 1), LANES * lane:LANES * (lane + 1)])
    return jnp.concatenate([scr[lane] for lane in range(cols // LANES)], axis=-1)


def _swap_index(n):
    per = ROW_GROUP // SUBLANES
    return np.arange(n).reshape(n // ROW_GROUP, SUBLANES, per).transpose(0, 2, 1).reshape(n)


def _ada_body(c_ref, w_ref, b_ref, o_ref):
    a = _silu(c_ref[...]).astype(BF16)
    o_ref[...] = _dot(a, w_ref[...].astype(BF16)) + b_ref[...]


def _ada(c_all, w_ada, b_ada):
    depth = w_ada.shape[0]
    n = c_all.shape[0]
    return pl.pallas_call(
        _ada_body,
        grid=(depth, N_MOD),
        in_specs=[
            pl.BlockSpec((n, D_MODEL), lambda i, j: (0, 0)),
            pl.BlockSpec((None, D_MODEL, D_MODEL), lambda i, j: (i, 0, j)),
            pl.BlockSpec((None, 1, D_MODEL), lambda i, j: (i, 0, j)),
        ],
        out_specs=pl.BlockSpec((None, n, D_MODEL), lambda i, j: (i, 0, j)),
        out_shape=jax.ShapeDtypeStruct((depth, n, N_MOD * D_MODEL), F32),
        compiler_params=_params(ADA_CALL_VMEM_MIB),
        name="ada",
    )(c_all, w_ada, b_ada.reshape(depth, 1, N_MOD * D_MODEL))


def _rotary(x, cos, sin):
    half = x.shape[-1] // 2
    x1, x2 = x[:, :half], x[:, half:]
    return jnp.concatenate([x1 * cos - x2 * sin, x1 * sin + x2 * cos], axis=-1)


def _ret_head(q, k, v, s_prev, cos, sin, mask, qdec, kdec, gs):
    qb = (_rotary(q, cos, sin) * (RET_DK ** -0.5)).astype(BF16)
    kr = _rotary(k, cos, sin)
    kb = kr.astype(BF16)
    vb = v.astype(BF16)
    scores = lax.dot_general(qb, kb, (((1,), (1,)), ((), ())),
                             preferred_element_type=F32) * mask
    o = _dot(scores.astype(BF16), vb) + qdec * _dot(qb, s_prev.astype(BF16))
    kd = (kr * kdec).T.astype(BF16)
    s_new = gs * s_prev + _dot(kd, vb)
    return o, s_new


def _ret_prompt_body(x_ref, mod_ref, g_ref, win_ref, cos_ref, sin_ref, mask_ref,
                     qdec_ref, kdec_ref, gs_ref, gng_ref, wout_ref,
                     xo_ref, so_ref, s_ref, *maybe_swap_ref):
    t = pl.program_id(1)

    @pl.when(t == 0)
    def _():
        s_ref[...] = jnp.zeros_like(s_ref)

    x = x_ref[...]
    if maybe_swap_ref:
        x = _swap_rows(x, maybe_swap_ref[0])
    mod = mod_ref[...]
    tm = x.shape[0]
    h = _modulated(x, g_ref[...], mod[:, 0:D_MODEL], mod[:, D_MODEL:2 * D_MODEL],
                   1, tm).astype(BF16)
    y = jnp.zeros((tm, D_MODEL), F32)
    for hd in range(RET_HEADS):
        q = _dot(h, win_ref[:, hd * RET_DK:(hd + 1) * RET_DK])
        k = _dot(h, win_ref[:, RET_QK + hd * RET_DK:RET_QK + (hd + 1) * RET_DK])
        v = _dot(h, win_ref[:, 2 * RET_QK + hd * RET_DV:2 * RET_QK + (hd + 1) * RET_DV])
        gate = _dot(h, win_ref[:, 2 * RET_QK + RET_V + hd * RET_DV:
                               2 * RET_QK + RET_V + (hd + 1) * RET_DV])
        state = s_ref[hd]
        blocks = []
        for r in range(0, tm, ROW_TILE):
            rows = slice(r, r + ROW_TILE)
            o, state = _ret_head(q[rows], k[rows], v[rows], state,
                                 cos_ref[rows, :], sin_ref[rows, :], mask_ref[hd],
                                 qdec_ref[:, hd:hd + 1], kdec_ref[:, hd:hd + 1],
                                 gs_ref[:, hd:hd + 1])
            blocks.append(o)
        s_ref[hd] = state
        o = blocks[0] if len(blocks) == 1 else jnp.concatenate(blocks, axis=0)
        on = _layernorm_nogain(o) * gng_ref[:, hd * RET_DV:(hd + 1) * RET_DV]
        gated = (_silu(gate) * on).astype(BF16)
        y = y + _dot(gated, wout_ref[hd * RET_DV:(hd + 1) * RET_DV, :D_MODEL])
    xo_ref[...] = x + mod[:, 2 * D_MODEL:3 * D_MODEL] * y

    @pl.when(t == pl.num_programs(1) - 1)
    def _():
        so_ref[...] = s_ref[...]


def _ret_prompt(x, mod, g, w_in, cos, sin, mask, qdec, kdec, gs, gn_g, w_out, swap_in,
                casts=()):
    b, seq, _ = x.shape
    tm = MIXER_ROW_TILE
    grid = (b, seq // tm)
    in_specs = [
        pl.BlockSpec((None, tm, D_MODEL), lambda i, t: (i, t, 0)),
        pl.BlockSpec((None, 1, N_MOD * D_MODEL), lambda i, t: (i, 0, 0)),
        _resident((1, D_MODEL)),
        _resident(w_in.shape),
        pl.BlockSpec((tm, RET_DK // 2), lambda i, t: (t, 0)),
        pl.BlockSpec((tm, RET_DK // 2), lambda i, t: (t, 0)),
        _resident(mask.shape),
        _resident(qdec.shape),
        _resident(kdec.shape),
        _resident(gs.shape),
        _resident((1, RET_V)),
        _resident(w_out.shape),
    ]
    out_specs = [
        pl.BlockSpec((None, tm, D_MODEL), lambda i, t: (i, t, 0)),
        pl.BlockSpec((None, RET_HEADS, RET_DK, RET_DV), lambda i, t: (i, 0, 0, 0)),
    ]
    out_shape = [
        jax.ShapeDtypeStruct(x.shape, F32),
        jax.ShapeDtypeStruct((b, RET_HEADS, RET_DK, RET_DV), F32),
    ]
    cast_in, cast_out, cast_shape = _cast_specs(casts, grid)
    return pl.pallas_call(
        _with_casts(_ret_prompt_body, len(in_specs), len(out_specs), len(casts)),
        grid=grid,
        in_specs=in_specs + cast_in,
        out_specs=out_specs + cast_out,
        out_shape=out_shape + cast_shape,
        scratch_shapes=[pltpu.VMEM((RET_HEADS, RET_DK, RET_DV), F32)] + (
            [pltpu.VMEM((D_MODEL // LANES, tm, LANES), F32)] if swap_in else []),
        compiler_params=_params(FUSED_CALL_VMEM_MIB),
        name="ret_prompt",
    )(x, mod.reshape(b, 1, -1), g, w_in, cos, sin, mask, qdec, kdec, gs, gn_g, w_out,
      *[w for w, _ in casts])


def _proj_body(s, t, x_ref, mod_ref, g_ref, w_ref, o_ref):
    mod = mod_ref[...]
    h = _modulated(x_ref[...], g_ref[...], mod[:, 0:D_MODEL],
                   mod[:, D_MODEL:2 * D_MODEL], s, t).astype(BF16)
    o_ref[...] = _dot(h, w_ref[:, :o_ref.shape[-1]])


def _proj_sample(x, mod, g, w, n, s, t):
    nb, tm, _ = x.shape
    return pl.pallas_call(
        functools.partial(_proj_body, s, t),
        grid=(nb, 1),
        in_specs=[
            pl.BlockSpec((None, tm, D_MODEL), lambda i, j: (i, 0, 0)),
            pl.BlockSpec((None, s, N_MOD * D_MODEL), lambda i, j: (i, 0, 0)),
            _resident((1, D_MODEL)),
            _resident(w.shape),
        ],
        out_specs=pl.BlockSpec((None, tm, n), lambda i, j: (i, 0, 0)),
        out_shape=jax.ShapeDtypeStruct((nb, tm, n), F32),
        compiler_params=_params(SMALL_CALL_VMEM_MIB),
        name="ret_sample_proj",
    )(x, mod, g, w)


def _ret_sample_core_body(p_ref, s0_ref, cos_ref, sin_ref, mask_ref, qdec_ref,
                          kdec_ref, gs_ref, gng_ref, o_ref, so_ref):
    cos = cos_ref[...]
    sin = sin_ref[...]
    for b in range(RET_SAMPLE_STREAMS):
        for hd in range(RET_HEADS):
            q = p_ref[b, :, hd * RET_DK:(hd + 1) * RET_DK]
            k = p_ref[b, :, RET_QK + hd * RET_DK:RET_QK + (hd + 1) * RET_DK]
            v = p_ref[b, :, 2 * RET_QK + hd * RET_DV:2 * RET_QK + (hd + 1) * RET_DV]
            gate = p_ref[b, :, 2 * RET_QK + RET_V + hd * RET_DV:
                         2 * RET_QK + RET_V + (hd + 1) * RET_DV]
            o, s_new = _ret_head(q, k, v, s0_ref[b, hd], cos, sin, mask_ref[hd],
                                 qdec_ref[:, hd:hd + 1], kdec_ref[:, hd:hd + 1],
                                 gs_ref[:, hd:hd + 1])
            so_ref[b, hd] = s_new
            on = _layernorm_nogain(o) * gng_ref[:, hd * RET_DV:(hd + 1) * RET_DV]
            o_ref[b, :, hd * RET_DV:(hd + 1) * RET_DV] = _silu(gate) * on


def _ret_sample_core(proj, s0, cos, sin, mask, qdec, kdec, gs, gn_g):
    b, t, n = proj.shape
    nb = RET_SAMPLE_STREAMS
    return pl.pallas_call(
        _ret_sample_core_body,
        grid=(b // nb, 1),
        in_specs=[
            pl.BlockSpec((nb, t, n), lambda i, j: (i, 0, 0)),
            pl.BlockSpec((nb, RET_HEADS, RET_DK, RET_DV), lambda i, j: (i, 0, 0, 0)),
            _resident(cos.shape),
            _resident(sin.shape),
            _resident(mask.shape),
            _resident(qdec.shape),
            _resident(kdec.shape),
            _resident(gs.shape),
            _resident((1, RET_V)),
        ],
        out_specs=[
            pl.BlockSpec((nb, t, RET_V), lambda i, j: (i, 0, 0)),
            pl.BlockSpec((nb, RET_HEADS, RET_DK, RET_DV), lambda i, j: (i, 0, 0, 0)),
        ],
        out_shape=[
            jax.ShapeDtypeStruct((b, t, RET_V), F32),
            jax.ShapeDtypeStruct(s0.shape, F32),
        ],
        compiler_params=_params(SMALL_CALL_VMEM_MIB),
        name="ret_sample_core",
    )(proj, s0, cos, sin, mask, qdec, kdec, gs, gn_g)


def _out_body(s, t, a_ref, x_ref, mod_ref, w_ref, o_ref):
    y = _dot(a_ref[...].astype(BF16), w_ref[:, :D_MODEL])
    gate = _rows(mod_ref[...][:, 2 * D_MODEL:3 * D_MODEL], s, t)
    o_ref[...] = x_ref[...] + gate * y


def _out_sample(a, x, mod, w, s, t):
    nb, tm, k = a.shape
    return pl.pallas_call(
        functools.partial(_out_body, s, t),
        grid=(nb, 1),
        in_specs=[
            pl.BlockSpec((None, tm, k), lambda i, j: (i, 0, 0)),
            pl.BlockSpec((None, tm, D_MODEL), lambda i, j: (i, 0, 0)),
            pl.BlockSpec((None, s, N_MOD * D_MODEL), lambda i, j: (i, 0, 0)),
            _resident(w.shape),
        ],
        out_specs=pl.BlockSpec((None, tm, D_MODEL), lambda i, j: (i, 0, 0)),
        out_shape=jax.ShapeDtypeStruct(x.shape, F32),
        compiler_params=_params(SMALL_CALL_VMEM_MIB),
        name="ret_sample_out",
    )(a, x, mod, w)


def _decay_tables(block, chunk):
    log_gamma = np.log(1.0 - np.exp2(-5.0 - np.arange(RET_HEADS)))
    idx = np.arange(block, dtype=np.float64)
    ch = np.arange(block) // chunk
    dist = np.abs(idx[:, None] - idx[None, :])
    decay = np.exp(log_gamma[:, None, None] * dist[None])
    mask = np.where((ch[None, :] <= ch[:, None])[None], decay, 0.0)
    qdec = np.exp((idx[:, None] + 1.0) * log_gamma[None, :])
    kdec = np.exp((block - 1.0 - idx)[:, None] * log_gamma[None, :])
    gs = np.exp(block * log_gamma)[None, :]
    return tuple(jnp.asarray(a, F32) for a in (mask, qdec, kdec, gs))


def _rope_tables(first, count):
    half = RET_DK // 2
    inv = np.power(ROPE_BASE, -np.arange(half, dtype=np.float64) / half)
    ang = (first + np.arange(count, dtype=np.float64))[:, None] * inv[None, :]
    return jnp.asarray(np.cos(ang), F32), jnp.asarray(np.sin(ang), F32)


def _sgu_body(s, t, emit_v, x_ref, mod_ref, g_ref, win_ref, lng_ref, lnb_ref,
              wmix_ref, bmix_ref, wout_ref, xo_ref, *maybe_v_ref):
    x = x_ref[...]
    mod = mod_ref[...]
    h = _modulated(x, g_ref[...], mod[:, 0:D_MODEL], mod[:, D_MODEL:2 * D_MODEL],
                   s, t).astype(BF16)
    groups = range(SGU_GROUPS)

    def cols(grp, base):
        return slice(base + grp * SGU_DG, base + (grp + 1) * SGU_DG)

    def proj(grp, base):
        return _dot(h, win_ref[:, cols(grp, base)])

    pv = [proj(0, SGU_D)]
    v = []
    for grp in groups:
        pv.append(proj(grp + 1, SGU_D) if grp + 1 < SGU_GROUPS else proj(0, 0))
        v.append(_gelu(pv[grp]))
    pu = [pv.pop(), proj(1, 0)]
    mu = sum(jnp.sum(vg, axis=-1, keepdims=True) for vg in v) * (1.0 / SGU_D)
    d = [vg - mu for vg in v]
    var = sum(jnp.sum(dg * dg, axis=-1, keepdims=True) for dg in d) * (1.0 / SGU_D)
    rs = lax.rsqrt(var + EPS)
    vb = []
    for grp in groups:
        vn = d[grp] * rs * lng_ref[:, cols(grp, 0)] + lnb_ref[:, cols(grp, 0)]
        if emit_v:
            maybe_v_ref[0][:, cols(grp, 0)] = vn
        vb.append(vn.astype(BF16))
    def mix(grp):
        parts = [_dot(wmix_ref[grp], vb[grp][r:r + ROW_TILE]) + bmix_ref[:, grp:grp + 1]
                 for r in range(0, s * t, ROW_TILE)]
        return parts[0] if len(parts) == 1 else jnp.concatenate(parts, axis=0)

    y = jnp.zeros((s * t, D_MODEL), F32)
    mixed = mix(0)
    for grp in groups:
        if grp + 2 < SGU_GROUPS:
            pu.append(proj(grp + 2, 0))
        u = _gelu(pu[grp])
        cur = mixed
        if grp + 1 < SGU_GROUPS:
            mixed = mix(grp + 1)
        gated = (u * cur).astype(BF16)
        y = y + _dot(gated, wout_ref[cols(grp, 0), :D_MODEL])
    xo_ref[...] = x + _rows(mod[:, 2 * D_MODEL:3 * D_MODEL], s, t) * y


def _sgu(x, mod, g, w_in, ln_g, ln_b, w_mix, b_mix, w_out, s, t, emit_v, casts=()):
    nb, r, _ = x.shape
    tm = s * t
    grid = (nb, r // tm)
    in_specs = [
        pl.BlockSpec((None, tm, D_MODEL), lambda i, j: (i, j, 0)),
        pl.BlockSpec((None, s, N_MOD * D_MODEL), lambda i, j: (i, 0, 0)),
        _resident((1, D_MODEL)),
        _resident(w_in.shape),
        _resident((1, SGU_D)),
        _resident((1, SGU_D)),
        _resident(w_mix.shape),
        _resident(b_mix.shape),
        _resident(w_out.shape),
    ]
    out_specs = [pl.BlockSpec((None, tm, D_MODEL), lambda i, j: (i, j, 0))]
    out_shape = [jax.ShapeDtypeStruct(x.shape, F32)]
    if emit_v:
        out_specs.append(pl.BlockSpec((None, tm, SGU_D), lambda i, j: (i, j, 0)))
        out_shape.append(jax.ShapeDtypeStruct((nb, r, SGU_D), F32))
    cast_in, cast_out, cast_shape = _cast_specs(casts, grid)
    return pl.pallas_call(
        _with_casts(functools.partial(_sgu_body, s, t, emit_v),
                    len(in_specs), len(out_specs), len(casts)),
        grid=grid,
        in_specs=in_specs + cast_in,
        out_specs=out_specs + cast_out,
        out_shape=out_shape + cast_shape,
        compiler_params=_params(FUSED_CALL_VMEM_MIB),
        name="sgu_sample" if emit_v else "sgu_prompt",
    )(x, mod, g, w_in, ln_g, ln_b, w_mix, b_mix, w_out, *[w for w, _ in casts])


def _sgu_mix_tables(w_s, b_s, seq_len):
    length = min(seq_len, SGU_CHUNK)
    blk = np.arange(SGU_CHUNK) // RET_CHUNK
    w = jnp.where(blk[None, :] <= blk[:, None], w_s, 0)[:, :length, :length]
    reps = ROW_TILE // length
    seg = np.arange(ROW_TILE) // length
    tiled = jnp.broadcast_to(w[:, None, :, None, :], (SGU_GROUPS, reps, length, reps, length))
    tile = jnp.where(seg[None, :] == seg[:, None], tiled.reshape(SGU_GROUPS, ROW_TILE, ROW_TILE), 0)
    bias = jnp.tile(b_s[:, :length].T, (reps, 1))
    return tile.astype(BF16), bias


def _ffn_body(s, t, final_norm, swap_out, x_ref, mod_ref, g_ref, wup_ref, cw_ref, cb_ref,
              wdn_ref, buf_ref, fg_ref, xo_ref, co_ref, *scratch):
    tm = s * t
    x = x_ref[...]
    mod = mod_ref[...]
    h = _modulated(x, g_ref[...], mod[:, 3 * D_MODEL:4 * D_MODEL],
                   mod[:, 4 * D_MODEL:5 * D_MODEL], s, t).astype(BF16)
    if s == 1:
        carry_ref = scratch[0]

        @pl.when(pl.program_id(1) == 0)
        def _():
            carry_ref[...] = buf_ref[0]

    def up(c, w):
        return [_dot(h, wup_ref[:, base + c:base + c + w]) for base in (0, FFN_D)]

    def shifted_swapped(a, lo, w):
        groups = tm // ROW_GROUP
        per = ROW_GROUP // SUBLANES
        a4 = a.reshape(groups, per, SUBLANES, w)
        p0 = carry_ref[0:1, lo:lo + w]
        p1 = carry_ref[1:2, lo:lo + w]
        carry_ref[0:1, lo:lo + w] = a4[groups - 1, per - 2, SUBLANES - 1:, :]
        carry_ref[1:2, lo:lo + w] = a4[groups - 1, per - 1, SUBLANES - 1:, :]
        sub = lax.broadcasted_iota(jnp.int32, (1, SUBLANES, 1), 1)

        def wrapped(col, before):
            r = pltpu.roll(col, 1, 1)
            prev = jnp.concatenate(
                [jnp.broadcast_to(before[None], (1, SUBLANES, w)), r[:-1]], axis=0)
            return jnp.where(sub == 0, prev, r)

        w1 = wrapped(a4[:, per - 1], p1)[:, None]
        w2 = wrapped(a4[:, per - 2], p0)[:, None]
        a1 = jnp.concatenate([w1, a4[:, :per - 1]], axis=1).reshape(tm, w)
        a2 = jnp.concatenate([w2, w1, a4[:, :per - 2]], axis=1).reshape(tm, w)
        return a1, a2

    def shifted_streams(a, lo, w):
        r1 = pltpu.roll(a, 1, 0)
        r2 = pltpu.roll(a, 2, 0)
        p0 = _rows(buf_ref[:, 0, lo:lo + w], s, t)
        p1 = _rows(buf_ref[:, 1, lo:lo + w], s, t)
        co_ref[:, :, lo:lo + w] = a.reshape(s, t, w)[:, t - 2:, :]
        pos = lax.broadcasted_iota(jnp.int32, (tm, 1), 0) & (t - 1)
        a1 = jnp.where(pos == 0, p1, r1)
        a2 = jnp.where(pos == 0, p0, jnp.where(pos == 1, p1, r2))
        return a1, a2

    def conv(a, lo, w):
        a1, a2 = (shifted_swapped if s == 1 else shifted_streams)(a, lo, w)
        c = cb_ref[:, lo:lo + w] + a2 * cw_ref[0:1, lo:lo + w]
        c = c + a1 * cw_ref[1:2, lo:lo + w]
        return c + a * cw_ref[2:3, lo:lo + w]

    y = jnp.zeros((tm, D_MODEL), F32)
    nxt = up(*FFN_CHUNKS[0])
    for k, (c, w) in enumerate(FFN_CHUNKS):
        cur = nxt
        if k + 1 < len(FFN_CHUNKS):
            nxt = up(*FFN_CHUNKS[k + 1])
        gate = conv(cur[0], c, w)
        val = conv(cur[1], FFN_D + c, w)
        hid = (_silu(gate) * val).astype(BF16)
        y = y + _dot(hid, wdn_ref[c:c + w, :D_MODEL])
    if s == 1:
        co_ref[0] = carry_ref[...]
    xn = x + _rows(mod[:, 5 * D_MODEL:6 * D_MODEL], s, t) * y
    if final_norm:
        xn = _rmsnorm(xn, fg_ref[...])
    if swap_out:
        xn = _swap_rows(xn, scratch[1])
    xo_ref[...] = xn


def _ffn(x, mod, g, w_up, conv_w, conv_b, w_down, buf, final_g, s, t, final_norm, swap_out,
         name, casts=()):
    nb, r, _ = x.shape
    tm = s * t
    assert t & (t - 1) == 0 and t >= CONV_W - 1 and (s > 1 or t % ROW_GROUP == 0)
    grid = (nb, r // tm)
    in_specs = [
        pl.BlockSpec((None, tm, D_MODEL), lambda i, j: (i, j, 0)),
        pl.BlockSpec((None, s, N_MOD * D_MODEL), lambda i, j: (i, 0, 0)),
        _resident((1, D_MODEL)),
        _resident(w_up.shape),
        _resident((CONV_W, 2 * FFN_D)),
        _resident((1, 2 * FFN_D)),
        _resident(w_down.shape),
        pl.BlockSpec((None, s, CONV_W - 1, 2 * FFN_D), lambda i, j: (i, 0, 0, 0)),
        _resident((1, D_MODEL)),
    ]
    out_specs = [
        pl.BlockSpec((None, tm, D_MODEL), lambda i, j: (i, j, 0)),
        pl.BlockSpec((None, s, CONV_W - 1, 2 * FFN_D), lambda i, j: (i, 0, 0, 0)),
    ]
    out_shape = [
        jax.ShapeDtypeStruct(x.shape, F32),
        jax.ShapeDtypeStruct(buf.shape, F32),
    ]
    cast_in, cast_out, cast_shape = _cast_specs(casts, grid)
    return pl.pallas_call(
        _with_casts(functools.partial(_ffn_body, s, t, final_norm, swap_out),
                    len(in_specs), len(out_specs), len(casts)),
        grid=grid,
        in_specs=in_specs + cast_in,
        out_specs=out_specs + cast_out,
        out_shape=out_shape + cast_shape,
        scratch_shapes=[] if s > 1 else [pltpu.VMEM((CONV_W - 1, 2 * FFN_D), F32)] + (
            [pltpu.VMEM((D_MODEL // LANES, tm, LANES), F32)] if swap_out else []),
        compiler_params=_params(FUSED_CALL_VMEM_MIB),
        name=name,
    )(x, mod, g, w_up, conv_w, conv_b, w_down, buf, final_g, *[w for w, _ in casts])


def kernel(x_prompt, x_sample, state_ret, state_ffn_conv, c_prompt, c_sample,
           w_ada, b_ada, norm_mix_g, norm_ffn_g,
           ret_w_in, ret_gn_g, ret_w_out,
           sgu_w_in, sgu_ln_g, sgu_ln_b, sgu_w_s, sgu_b_s, sgu_w_out,
           ffn_w_up, ffn_conv_w, ffn_conv_b, ffn_w_down, final_g):
    bp, seq, _ = x_prompt.shape
    bs, dec, _ = x_sample.shape
    depth = w_ada.shape[0]
    ss = SAMPLE_ROW_TILE // dec
    nbs = bs // ss

    mod = _ada(jnp.concatenate([c_prompt, c_sample], axis=0), w_ada, b_ada)
    mod_p = mod[:, :bp].reshape(depth, bp, 1, -1)
    mod_s = mod[:, bp:].reshape(depth, nbs, ss, -1)

    xp = x_prompt
    xs = x_sample.reshape(nbs, SAMPLE_ROW_TILE, D_MODEL)
    fg = final_g.reshape(1, D_MODEL)
    zero_buf = jnp.zeros((bp, 1, CONV_W - 1, 2 * FFN_D), F32)
    swap_seq = _swap_index(seq)
    swap_tile = _swap_index(ROW_TILE)

    def mixer_weights(layer):
        src = (ret_w_in, ret_w_out) if layer % 2 == 0 else (sgu_w_in, sgu_w_out)
        return [(w, layer // 2) for w in src]

    w_in, w_out = [_to_resident_bf16(w[j]) for w, j in mixer_weights(0)]

    ret_p, ret_s, conv_p, conv_s, sgu_s = [], [], [], [], []
    for i in range(depth):
        j = i // 2
        g_mix = norm_mix_g[i].reshape(1, D_MODEL)
        ffn_casts = [(ffn_w_up, i), (ffn_w_down, i)]
        if i % 2 == 0:
            gn_g = ret_gn_g[j].reshape(1, RET_V)
            cos_p, sin_p = [tab[swap_seq] for tab in _rope_tables(0, seq)]
            mask, qdec, kdec, gs = _decay_tables(ROW_TILE, RET_CHUNK)
            xp, sp, w_up, w_down = _ret_prompt(
                xp, mod_p[i], g_mix, w_in, cos_p, sin_p,
                mask[:, swap_tile][:, :, swap_tile], qdec[swap_tile], kdec[swap_tile], gs,
                gn_g, w_out, i == 0, casts=ffn_casts)
            cos_s, sin_s = _rope_tables(PAST_LEN, dec)
            proj = _proj_sample(xs, mod_s[i], g_mix, w_in, 2 * RET_QK + 2 * RET_V, ss, dec)
            gated, s_new = _ret_sample_core(
                proj.reshape(bs, dec, -1), state_ret[j], cos_s, sin_s,
                *_decay_tables(dec, min(dec, RET_CHUNK)), gn_g)
            xs = _out_sample(gated.reshape(nbs, SAMPLE_ROW_TILE, RET_V), xs, mod_s[i],
                             w_out, ss, dec)
            ret_p.append(sp)
            ret_s.append(s_new)
        else:
            ln_g = sgu_ln_g[j].reshape(1, SGU_D)
            ln_b = sgu_ln_b[j].reshape(1, SGU_D)
            wm_p, bm_p = _sgu_mix_tables(sgu_w_s[j], sgu_b_s[j], seq)
            wm_s, bm_s = _sgu_mix_tables(sgu_w_s[j], sgu_b_s[j], dec)
            xp, w_up, w_down = _sgu(
                xp, mod_p[i], g_mix, w_in, ln_g, ln_b,
                wm_p[:, swap_tile][:, :, swap_tile], bm_p[swap_tile], w_out,
                1, MIXER_ROW_TILE, False, casts=ffn_casts)
            xs, vs = _sgu(xs, mod_s[i], g_mix, w_in, ln_g, ln_b, wm_s, bm_s, w_out,
                          ss, dec, True)
            sgu_s.append(vs.reshape(bs, dec, SGU_D))
        g_ffn = norm_ffn_g[i].reshape(1, D_MODEL)
        conv_b = ffn_conv_b[i].reshape(1, 2 * FFN_D)
        last = i == depth - 1
        xp, cp, *next_mixer = _ffn(
            xp, mod_p[i], g_ffn, w_up, ffn_conv_w[i], conv_b, w_down, zero_buf, fg,
            1, FFN_ROW_TILE, last, last, "ffn_prompt_%d" % i,
            casts=[] if last else mixer_weights(i + 1))
        xs, cs = _ffn(xs, mod_s[i], g_ffn, w_up, ffn_conv_w[i], conv_b, w_down,
                      state_ffn_conv[i].reshape(nbs, ss, CONV_W - 1, 2 * FFN_D),
                      fg, ss, dec, last, False, "ffn_sample_%d" % i)
        if not last:
            w_in, w_out = next_mixer
        conv_p.append(cp.reshape(bp, CONV_W - 1, 2 * FFN_D))
        conv_s.append(cs.reshape(bs, CONV_W - 1, 2 * FFN_D))

    return (xp, xs.reshape(bs, dec, D_MODEL), jnp.stack(ret_p), jnp.stack(ret_s),
            jnp.stack(conv_p), jnp.stack(conv_s), jnp.stack(sgu_s))
```

```python
import functools

import jax
import jax.numpy as jnp
import numpy as np
from jax import lax
from jax.experimental import pallas as pl
from jax.experimental.pallas import tpu as pltpu

D_MODEL = 1024
N_MOD = 6
RET_HEADS = 4
RET_DK = 256
RET_DV = 512
RET_QK = RET_HEADS * RET_DK
RET_V = RET_HEADS * RET_DV
RET_CHUNK = 64
ROPE_BASE = 10000.0
PAST_LEN = 1024
SGU_CHUNK = 128
SGU_GROUPS = 4
SGU_D = 3 * D_MODEL
SGU_DG = SGU_D // SGU_GROUPS
FFN_D = 2816
CONV_W = 3
EPS = 1e-6

ROW_TILE = 256
FFN_ROW_TILE = 512
MIXER_ROW_TILE = 512
SAMPLE_ROW_TILE = 512
RET_SAMPLE_STREAMS = 4
FFN_CHUNKS = ((0, 1024), (1024, 1024), (2048, 768))
MIB = 1024 * 1024
FUSED_CALL_VMEM_MIB = 56
SMALL_CALL_VMEM_MIB = 48
LIGHT_CALL_VMEM_MIB = 32
LANES = 128
SUBLANES = 8
ROW_GROUP = SUBLANES * SUBLANES
LOG2_E = float(np.log2(np.e))
BF16_SUBLANES = 16

BF16 = jnp.bfloat16
F32 = jnp.float32


def _dot(a, b):
    return jnp.dot(a, b, preferred_element_type=F32)


def _resident(shape):
    nd = len(shape)
    return pl.BlockSpec(shape, lambda *_: (0,) * nd, pipeline_mode=pl.Buffered(1))


def _padded_cols(cols):
    return cols + LANES if cols % (8 * LANES) == 0 else cols


def _to_resident_bf16(w):
    rows, cols = w.shape
    return jnp.pad(w.astype(BF16), ((0, 0), (0, _padded_cols(cols) - cols)))


def _cast_specs(jobs, grid):
    steps = grid[0] * grid[1]
    in_specs, out_specs, out_shape = [], [], []
    for w, layer in jobs:
        _, rows, cols = w.shape
        parts = max(p for p in range(1, steps + 1)
                    if rows % p == 0 and (rows // p) % BF16_SUBLANES == 0)

        def slab(i, j, parts=parts):
            return jnp.minimum(i * grid[1] + j, parts - 1)

        in_specs.append(pl.BlockSpec((None, rows // parts, cols),
                                     lambda i, j, layer=layer, slab=slab: (layer, slab(i, j), 0)))
        out_specs.append(pl.BlockSpec((rows // parts, _padded_cols(cols)),
                                      lambda i, j, slab=slab: (slab(i, j), 0)))
        out_shape.append(jax.ShapeDtypeStruct((rows, _padded_cols(cols)), BF16))
    return in_specs, out_specs, out_shape


def _with_casts(body, n_in, n_out, n_cast):
    def wrapped(*refs):
        ins, refs = refs[:n_in], refs[n_in:]
        cast_in, refs = refs[:n_cast], refs[n_cast:]
        outs, refs = refs[:n_out], refs[n_out:]
        cast_out, scratch = refs[:n_cast], refs[n_cast:]
        for src, dst in zip(cast_in, cast_out):
            rows, cols = src.shape
            dst[:, :cols] = src[...].astype(BF16)
            if dst.shape[1] > cols:
                dst[:, cols:] = jnp.zeros((rows, dst.shape[1] - cols), BF16)
        body(*ins, *outs, *scratch)
    return wrapped


def _params(vmem_mib):
    return pltpu.CompilerParams(
        dimension_semantics=("arbitrary", "arbitrary"),
        vmem_limit_bytes=vmem_mib * MIB)


def _rows(v, s, t):
    if s == 1:
        return v
    return jnp.broadcast_to(v[:, None, :], (s, t, v.shape[-1])).reshape(s * t, v.shape[-1])


def _rmsnorm(x, g):
    ms = jnp.mean(x * x, axis=-1, keepdims=True)
    return x * lax.rsqrt(ms + EPS) * g


def _modulated(x, g, shift, scale, s, t):
    return _rmsnorm(x, g) * (1.0 + _rows(scale, s, t)) + _rows(shift, s, t)


def _silu(x):
    return x / (1.0 + jnp.exp2(x * -LOG2_E))


def _gelu(x):
    k = -2.0 * (2.0 / np.pi) ** 0.5 * LOG2_E
    return x / (1.0 + jnp.exp2(x * (k + (k * 0.044715) * (x * x))))


def _layernorm_nogain(o):
    mu = jnp.mean(o, axis=-1, keepdims=True)
    d = o - mu
    var = jnp.mean(d * d, axis=-1, keepdims=True)
    return d * lax.rsqrt(var + EPS)


def _swap_rows(val, scr):
    rows, cols = val.shape
    per = ROW_GROUP // SUBLANES
    for lane in range(cols // LANES):
        for k in range(rows // SUBLANES):
            grp, v = divmod(k, per)
            scr.at[lane][pl.ds(ROW_GROUP * grp + v, SUBLANES, stride=per), :] = (
                val[SUBLANES * k:SUBLANES * (k + 1), LANES * lane:LANES * (lane + 1)])
    return jnp.concatenate([scr[lane] for lane in range(cols // LANES)], axis=-1)


def _swap_index(n):
    per = ROW_GROUP // SUBLANES
    return np.arange(n).reshape(n // ROW_GROUP, SUBLANES, per).transpose(0, 2, 1).reshape(n)


def _ada_body(c_ref, w_ref, b_ref, o_ref):
    a = _silu(c_ref[...]).astype(BF16)
    o_ref[...] = _dot(a, w_ref[...].astype(BF16)) + b_ref[...]


def _ada(c_all, w_ada, b_ada):
    depth = w_ada.shape[0]
    n = c_all.shape[0]
    return pl.pallas_call(
        _ada_body,
        grid=(depth, N_MOD),
        in_specs=[
            pl.BlockSpec((n, D_MODEL), lambda i, j: (0, 0)),
            pl.BlockSpec((None, D_MODEL, D_MODEL), lambda i, j: (i, 0, j)),
            pl.BlockSpec((None, 1, D_MODEL), lambda i, j: (i, 0, j)),
        ],
        out_specs=pl.BlockSpec((None, n, D_MODEL), lambda i, j: (i, 0, j)),
        out_shape=jax.ShapeDtypeStruct((depth, n, N_MOD * D_MODEL), F32),
        compiler_params=_params(LIGHT_CALL_VMEM_MIB),
        name="ada",
    )(c_all, w_ada, b_ada.reshape(depth, 1, N_MOD * D_MODEL))


def _rotary(x, cos, sin):
    half = x.shape[-1] // 2
    x1, x2 = x[:, :half], x[:, half:]
    return jnp.concatenate([x1 * cos - x2 * sin, x1 * sin + x2 * cos], axis=-1)


def _ret_head(q, k, v, s_prev, cos, sin, mask, qdec, kdec, gs):
    qb = (_rotary(q, cos, sin) * (RET_DK ** -0.5)).astype(BF16)
    kr = _rotary(k, cos, sin)
    kb = kr.astype(BF16)
    vb = v.astype(BF16)
    scores = lax.dot_general(qb, kb, (((1,), (1,)), ((), ())),
                             preferred_element_type=F32) * mask
    o = _dot(scores.astype(BF16), vb) + qdec * _dot(qb, s_prev.astype(BF16))
    kd = (kr * kdec).T.astype(BF16)
    s_new = gs * s_prev + _dot(kd, vb)
    return o, s_new


def _ret_prompt_body(x_ref, mod_ref, g_ref, win_ref, cos_ref, sin_ref, mask_ref,
                     qdec_ref, kdec_ref, gs_ref, gng_ref, wout_ref,
                     xo_ref, so_ref, s_ref, *maybe_swap_ref):
    t = pl.program_id(1)

    @pl.when(t == 0)
    def _():
        s_ref[...] = jnp.zeros_like(s_ref)

    x = x_ref[...]
    if maybe_swap_ref:
        x = _swap_rows(x, maybe_swap_ref[0])
    mod = mod_ref[...]
    tm = x.shape[0]
    h = _modulated(x, g_ref[...], mod[:, 0:D_MODEL], mod[:, D_MODEL:2 * D_MODEL],
                   1, tm).astype(BF16)
    y = jnp.zeros((tm, D_MODEL), F32)
    for hd in range(RET_HEADS):
        q = _dot(h, win_ref[:, hd * RET_DK:(hd + 1) * RET_DK])
        k = _dot(h, win_ref[:, RET_QK + hd * RET_DK:RET_QK + (hd + 1) * RET_DK])
        v = _dot(h, win_ref[:, 2 * RET_QK + hd * RET_DV:2 * RET_QK + (hd + 1) * RET_DV])
        gate = _dot(h, win_ref[:, 2 * RET_QK + RET_V + hd * RET_DV:
                               2 * RET_QK + RET_V + (hd + 1) * RET_DV])
        state = s_ref[hd]
        blocks = []
        for r in range(0, tm, ROW_TILE):
            rows = slice(r, r + ROW_TILE)
            o, state = _ret_head(q[rows], k[rows], v[rows], state,
                                 cos_ref[rows, :], sin_ref[rows, :], mask_ref[hd],
                                 qdec_ref[:, hd:hd + 1], kdec_ref[:, hd:hd + 1],
                                 gs_ref[:, hd:hd + 1])
            blocks.append(o)
        s_ref[hd] = state
        o = blocks[0] if len(blocks) == 1 else jnp.concatenate(blocks, axis=0)
        on = _layernorm_nogain(o) * gng_ref[:, hd * RET_DV:(hd + 1) * RET_DV]
        gated = (_silu(gate) * on).astype(BF16)
        y = y + _dot(gated, wout_ref[hd * RET_DV:(hd + 1) * RET_DV, :D_MODEL])
    xo_ref[...] = x + mod[:, 2 * D_MODEL:3 * D_MODEL] * y

    @pl.when(t == pl.num_programs(1) - 1)
    def _():
        so_ref[...] = s_ref[...]


def _ret_prompt(x, mod, g, w_in, cos, sin, mask, qdec, kdec, gs, gn_g, w_out, swap_in,
                casts=()):
    b, seq, _ = x.shape
    tm = MIXER_ROW_TILE
    grid = (b, seq // tm)
    in_specs = [
        pl.BlockSpec((None, tm, D_MODEL), lambda i, t: (i, t, 0)),
        pl.BlockSpec((None, 1, N_MOD * D_MODEL), lambda i, t: (i, 0, 0)),
        _resident((1, D_MODEL)),
        _resident(w_in.shape),
        pl.BlockSpec((tm, RET_DK // 2), lambda i, t: (t, 0)),
        pl.BlockSpec((tm, RET_DK // 2), lambda i, t: (t, 0)),
        _resident(mask.shape),
        _resident(qdec.shape),
        _resident(kdec.shape),
        _resident(gs.shape),
        _resident((1, RET_V)),
        _resident(w_out.shape),
    ]
    out_specs = [
        pl.BlockSpec((None, tm, D_MODEL), lambda i, t: (i, t, 0)),
        pl.BlockSpec((None, RET_HEADS, RET_DK, RET_DV), lambda i, t: (i, 0, 0, 0)),
    ]
    out_shape = [
        jax.ShapeDtypeStruct(x.shape, F32),
        jax.ShapeDtypeStruct((b, RET_HEADS, RET_DK, RET_DV), F32),
    ]
    cast_in, cast_out, cast_shape = _cast_specs(casts, grid)
    return pl.pallas_call(
        _with_casts(_ret_prompt_body, len(in_specs), len(out_specs), len(casts)),
        grid=grid,
        in_specs=in_specs + cast_in,
        out_specs=out_specs + cast_out,
        out_shape=out_shape + cast_shape,
        scratch_shapes=[pltpu.VMEM((RET_HEADS, RET_DK, RET_DV), F32)] + (
            [pltpu.VMEM((D_MODEL // LANES, tm, LANES), F32)] if swap_in else []),
        compiler_params=_params(FUSED_CALL_VMEM_MIB),
        name="ret_prompt",
    )(x, mod.reshape(b, 1, -1), g, w_in, cos, sin, mask, qdec, kdec, gs, gn_g, w_out,
      *[w for w, _ in casts])


def _proj_body(s, t, x_ref, mod_ref, g_ref, w_ref, o_ref):
    mod = mod_ref[...]
    h = _modulated(x_ref[...], g_ref[...], mod[:, 0:D_MODEL],
                   mod[:, D_MODEL:2 * D_MODEL], s, t).astype(BF16)
    o_ref[...] = _dot(h, w_ref[:, :o_ref.shape[-1]])


def _proj_sample(x, mod, g, w, n, s, t):
    nb, tm, _ = x.shape
    return pl.pallas_call(
        functools.partial(_proj_body, s, t),
        grid=(nb, 1),
        in_specs=[
            pl.BlockSpec((None, tm, D_MODEL), lambda i, j: (i, 0, 0)),
            pl.BlockSpec((None, s, N_MOD * D_MODEL), lambda i, j: (i, 0, 0)),
            _resident((1, D_MODEL)),
            _resident(w.shape),
        ],
        out_specs=pl.BlockSpec((None, tm, n), lambda i, j: (i, 0, 0)),
        out_shape=jax.ShapeDtypeStruct((nb, tm, n), F32),
        compiler_params=_params(SMALL_CALL_VMEM_MIB),
        name="ret_sample_proj",
    )(x, mod, g, w)


def _ret_sample_core_body(p_ref, s0_ref, cos_ref, sin_ref, mask_ref, qdec_ref,
                          kdec_ref, gs_ref, gng_ref, o_ref, so_ref):
    cos = cos_ref[...]
    sin = sin_ref[...]
    for b in range(RET_SAMPLE_STREAMS):
        for hd in range(RET_HEADS):
            q = p_ref[b, :, hd * RET_DK:(hd + 1) * RET_DK]
            k = p_ref[b, :, RET_QK + hd * RET_DK:RET_QK + (hd + 1) * RET_DK]
            v = p_ref[b, :, 2 * RET_QK + hd * RET_DV:2 * RET_QK + (hd + 1) * RET_DV]
            gate = p_ref[b, :, 2 * RET_QK + RET_V + hd * RET_DV:
                         2 * RET_QK + RET_V + (hd + 1) * RET_DV]
            o, s_new = _ret_head(q, k, v, s0_ref[b, hd], cos, sin, mask_ref[hd],
                                 qdec_ref[:, hd:hd + 1], kdec_ref[:, hd:hd + 1],
                                 gs_ref[:, hd:hd + 1])
            so_ref[b, hd] = s_new
            on = _layernorm_nogain(o) * gng_ref[:, hd * RET_DV:(hd + 1) * RET_DV]
            o_ref[b, :, hd * RET_DV:(hd + 1) * RET_DV] = _silu(gate) * on


def _ret_sample_core(proj, s0, cos, sin, mask, qdec, kdec, gs, gn_g):
    b, t, n = proj.shape
    nb = RET_SAMPLE_STREAMS
    return pl.pallas_call(
        _ret_sample_core_body,
        grid=(b // nb, 1),
        in_specs=[
            pl.BlockSpec((nb, t, n), lambda i, j: (i, 0, 0)),
            pl.BlockSpec((nb, RET_HEADS, RET_DK, RET_DV), lambda i, j: (i, 0, 0, 0)),
            _resident(cos.shape),
            _resident(sin.shape),
            _resident(mask.shape),
            _resident(qdec.shape),
            _resident(kdec.shape),
            _resident(gs.shape),
            _resident((1, RET_V)),
        ],
        out_specs=[
            pl.BlockSpec((nb, t, RET_V), lambda i, j: (i, 0, 0)),
            pl.BlockSpec((nb, RET_HEADS, RET_DK, RET_DV), lambda i, j: (i, 0, 0, 0)),
        ],
        out_shape=[
            jax.ShapeDtypeStruct((b, t, RET_V), F32),
            jax.ShapeDtypeStruct(s0.shape, F32),
        ],
        compiler_params=_params(SMALL_CALL_VMEM_MIB),
        name="ret_sample_core",
    )(proj, s0, cos, sin, mask, qdec, kdec, gs, gn_g)


def _out_body(s, t, a_ref, x_ref, mod_ref, w_ref, o_ref):
    y = _dot(a_ref[...].astype(BF16), w_ref[:, :D_MODEL])
    gate = _rows(mod_ref[...][:, 2 * D_MODEL:3 * D_MODEL], s, t)
    o_ref[...] = x_ref[...] + gate * y


def _out_sample(a, x, mod, w, s, t):
    nb, tm, k = a.shape
    return pl.pallas_call(
        functools.partial(_out_body, s, t),
        grid=(nb, 1),
        in_specs=[
            pl.BlockSpec((None, tm, k), lambda i, j: (i, 0, 0)),
            pl.BlockSpec((None, tm, D_MODEL), lambda i, j: (i, 0, 0)),
            pl.BlockSpec((None, s, N_MOD * D_MODEL), lambda i, j: (i, 0, 0)),
            _resident(w.shape),
        ],
        out_specs=pl.BlockSpec((None, tm, D_MODEL), lambda i, j: (i, 0, 0)),
        out_shape=jax.ShapeDtypeStruct(x.shape, F32),
        compiler_params=_params(LIGHT_CALL_VMEM_MIB),
        name="ret_sample_out",
    )(a, x, mod, w)


def _decay_tables(block, chunk):
    log_gamma = np.log(1.0 - np.exp2(-5.0 - np.arange(RET_HEADS)))
    idx = np.arange(block, dtype=np.float64)
    ch = np.arange(block) // chunk
    dist = np.abs(idx[:, None] - idx[None, :])
    decay = np.exp(log_gamma[:, None, None] * dist[None])
    mask = np.where((ch[None, :] <= ch[:, None])[None], decay, 0.0)
    qdec = np.exp((idx[:, None] + 1.0) * log_gamma[None, :])
    kdec = np.exp((block - 1.0 - idx)[:, None] * log_gamma[None, :])
    gs = np.exp(block * log_gamma)[None, :]
    return tuple(jnp.asarray(a, F32) for a in (mask, qdec, kdec, gs))


def _rope_tables(first, count):
    half = RET_DK // 2
    inv = np.power(ROPE_BASE, -np.arange(half, dtype=np.float64) / half)
    ang = (first + np.arange(count, dtype=np.float64))[:, None] * inv[None, :]
    return jnp.asarray(np.cos(ang), F32), jnp.asarray(np.sin(ang), F32)


def _sgu_body(s, t, emit_v, x_ref, mod_ref, g_ref, win_ref, lng_ref, lnb_ref,
              wmix_ref, bmix_ref, wout_ref, xo_ref, *maybe_v_ref):
    x = x_ref[...]
    mod = mod_ref[...]
    h = _modulated(x, g_ref[...], mod[:, 0:D_MODEL], mod[:, D_MODEL:2 * D_MODEL],
                   s, t).astype(BF16)
    groups = range(SGU_GROUPS)

    def cols(grp, base):
        return slice(base + grp * SGU_DG, base + (grp + 1) * SGU_DG)

    def proj(grp, base):
        return _dot(h, win_ref[:, cols(grp, base)])

    pv = [proj(0, SGU_D)]
    v = []
    for grp in groups:
        pv.append(proj(grp + 1, SGU_D) if grp + 1 < SGU_GROUPS else proj(0, 0))
        v.append(_gelu(pv[grp]))
    pu = [pv.pop(), proj(1, 0)]
    mu = sum(jnp.sum(vg, axis=-1, keepdims=True) for vg in v) * (1.0 / SGU_D)
    d = [vg - mu for vg in v]
    var = sum(jnp.sum(dg * dg, axis=-1, keepdims=True) for dg in d) * (1.0 / SGU_D)
    rs = lax.rsqrt(var + EPS)
    vb = []
    for grp in groups:
        vn = d[grp] * rs * lng_ref[:, cols(grp, 0)] + lnb_ref[:, cols(grp, 0)]
        if emit_v:
            maybe_v_ref[0][:, cols(grp, 0)] = vn
        vb.append(vn.astype(BF16))
    def mix(grp):
        parts = [_dot(wmix_ref[grp], vb[grp][r:r + ROW_TILE]) + bmix_ref[:, grp:grp + 1]
                 for r in range(0, s * t, ROW_TILE)]
        return parts[0] if len(parts) == 1 else jnp.concatenate(parts, axis=0)

    y = jnp.zeros((s * t, D_MODEL), F32)
    mixed = mix(0)
    for grp in groups:
        if grp + 2 < SGU_GROUPS:
            pu.append(proj(grp + 2, 0))
        u = _gelu(pu[grp])
        cur = mixed
        if grp + 1 < SGU_GROUPS:
            mixed = mix(grp + 1)
        gated = (u * cur).astype(BF16)
        y = y + _dot(gated, wout_ref[cols(grp, 0), :D_MODEL])
    xo_ref[...] = x + _rows(mod[:, 2 * D_MODEL:3 * D_MODEL], s, t) * y


def _sgu(x, mod, g, w_in, ln_g, ln_b, w_mix, b_mix, w_out, s, t, emit_v, casts=()):
    nb, r, _ = x.shape
    tm = s * t
    grid = (nb, r // tm)
    in_specs = [
        pl.BlockSpec((None, tm, D_MODEL), lambda i, j: (i, j, 0)),
        pl.BlockSpec((None, s, N_MOD * D_MODEL), lambda i, j: (i, 0, 0)),
        _resident((1, D_MODEL)),
        _resident(w_in.shape),
        _resident((1, SGU_D)),
        _resident((1, SGU_D)),
        _resident(w_mix.shape),
        _resident(b_mix.shape),
        _resident(w_out.shape),
    ]
    out_specs = [pl.BlockSpec((None, tm, D_MODEL), lambda i, j: (i, j, 0))]
    out_shape = [jax.ShapeDtypeStruct(x.shape, F32)]
    if emit_v:
        out_specs.append(pl.BlockSpec((None, tm, SGU_D), lambda i, j: (i, j, 0)))
        out_shape.append(jax.ShapeDtypeStruct((nb, r, SGU_D), F32))
    cast_in, cast_out, cast_shape = _cast_specs(casts, grid)
    return pl.pallas_call(
        _with_casts(functools.partial(_sgu_body, s, t, emit_v),
                    len(in_specs), len(out_specs), len(casts)),
        grid=grid,
        in_specs=in_specs + cast_in,
        out_specs=out_specs + cast_out,
        out_shape=out_shape + cast_shape,
        compiler_params=_params(FUSED_CALL_VMEM_MIB),
        name="sgu_sample" if emit_v else "sgu_prompt",
    )(x, mod, g, w_in, ln_g, ln_b, w_mix, b_mix, w_out, *[w for w, _ in casts])


def _sgu_mix_tables(w_s, b_s, seq_len):
    length = min(seq_len, SGU_CHUNK)
    blk = np.arange(SGU_CHUNK) // RET_CHUNK
    w = jnp.where(blk[None, :] <= blk[:, None], w_s, 0)[:, :length, :length]
    reps = ROW_TILE // length
    seg = np.arange(ROW_TILE) // length
    tiled = jnp.broadcast_to(w[:, None, :, None, :], (SGU_GROUPS, reps, length, reps, length))
    tile = jnp.where(seg[None, :] == seg[:, None], tiled.reshape(SGU_GROUPS, ROW_TILE, ROW_TILE), 0)
    bias = jnp.tile(b_s[:, :length].T, (reps, 1))
    return tile.astype(BF16), bias


def _ffn_body(s, t, final_norm, swap_out, x_ref, mod_ref, g_ref, wup_ref, cw_ref, cb_ref,
              wdn_ref, buf_ref, fg_ref, xo_ref, co_ref, *scratch):
    tm = s * t
    x = x_ref[...]
    mod = mod_ref[...]
    h = _modulated(x, g_ref[...], mod[:, 3 * D_MODEL:4 * D_MODEL],
                   mod[:, 4 * D_MODEL:5 * D_MODEL], s, t).astype(BF16)
    if s == 1:
        carry_ref = scratch[0]

        @pl.when(pl.program_id(1) == 0)
        def _():
            carry_ref[...] = buf_ref[0]

    def up(c, w):
        return [_dot(h, wup_ref[:, base + c:base + c + w]) for base in (0, FFN_D)]

    def shifted_swapped(a, lo, w):
        groups = tm // ROW_GROUP
        per = ROW_GROUP // SUBLANES
        a4 = a.reshape(groups, per, SUBLANES, w)
        p0 = carry_ref[0:1, lo:lo + w]
        p1 = carry_ref[1:2, lo:lo + w]
        carry_ref[0:1, lo:lo + w] = a4[groups - 1, per - 2, SUBLANES - 1:, :]
        carry_ref[1:2, lo:lo + w] = a4[groups - 1, per - 1, SUBLANES - 1:, :]
        sub = lax.broadcasted_iota(jnp.int32, (1, SUBLANES, 1), 1)

        def wrapped(col, before):
            r = pltpu.roll(col, 1, 1)
            prev = jnp.concatenate(
                [jnp.broadcast_to(before[None], (1, SUBLANES, w)), r[:-1]], axis=0)
            return jnp.where(sub == 0, prev, r)

        w1 = wrapped(a4[:, per - 1], p1)[:, None]
        w2 = wrapped(a4[:, per - 2], p0)[:, None]
        a1 = jnp.concatenate([w1, a4[:, :per - 1]], axis=1).reshape(tm, w)
        a2 = jnp.concatenate([w2, w1, a4[:, :per - 2]], axis=1).reshape(tm, w)
        return a1, a2

    def shifted_streams(a, lo, w):
        r1 = pltpu.roll(a, 1, 0)
        r2 = pltpu.roll(a, 2, 0)
        p0 = _rows(buf_ref[:, 0, lo:lo + w], s, t)
        p1 = _rows(buf_ref[:, 1, lo:lo + w], s, t)
        co_ref[:, :, lo:lo + w] = a.reshape(s, t, w)[:, t - 2:, :]
        pos = lax.broadcasted_iota(jnp.int32, (tm, 1), 0) & (t - 1)
        a1 = jnp.where(pos == 0, p1, r1)
        a2 = jnp.where(pos == 0, p0, jnp.where(pos == 1, p1, r2))
        return a1, a2

    def conv(a, lo, w):
        a1, a2 = (shifted_swapped if s == 1 else shifted_streams)(a, lo, w)
        c = cb_ref[:, lo:lo + w] + a2 * cw_ref[0:1, lo:lo + w]
        c = c + a1 * cw_ref[1:2, lo:lo + w]
        return c + a * cw_ref[2:3, lo:lo + w]

    y = jnp.zeros((tm, D_MODEL), F32)
    nxt = up(*FFN_CHUNKS[0])
    for k, (c, w) in enumerate(FFN_CHUNKS):
        cur = nxt
        if k + 1 < len(FFN_CHUNKS):
            nxt = up(*FFN_CHUNKS[k + 1])
        gate = conv(cur[0], c, w)
        val = conv(cur[1], FFN_D + c, w)
        hid = (_silu(gate) * val).astype(BF16)
        y = y + _dot(hid, wdn_ref[c:c + w, :D_MODEL])
    if s == 1:
        co_ref[0] = carry_ref[...]
    xn = x + _rows(mod[:, 5 * D_MODEL:6 * D_MODEL], s, t) * y
    if final_norm:
        xn = _rmsnorm(xn, fg_ref[...])
    if swap_out:
        xn = _swap_rows(xn, scratch[1])
    xo_ref[...] = xn


def _ffn(x, mod, g, w_up, conv_w, conv_b, w_down, buf, final_g, s, t, final_norm, swap_out,
         name, casts=()):
    nb, r, _ = x.shape
    tm = s * t
    assert t & (t - 1) == 0 and t >= CONV_W - 1 and (s > 1 or t % ROW_GROUP == 0)
    grid = (nb, r // tm)
    in_specs = [
        pl.BlockSpec((None, tm, D_MODEL), lambda i, j: (i, j, 0)),
        pl.BlockSpec((None, s, N_MOD * D_MODEL), lambda i, j: (i, 0, 0)),
        _resident((1, D_MODEL)),
        _resident(w_up.shape),
        _resident((CONV_W, 2 * FFN_D)),
        _resident((1, 2 * FFN_D)),
        _resident(w_down.shape),
        pl.BlockSpec((None, s, CONV_W - 1, 2 * FFN_D), lambda i, j: (i, 0, 0, 0)),
        _resident((1, D_MODEL)),
    ]
    out_specs = [
        pl.BlockSpec((None, tm, D_MODEL), lambda i, j: (i, j, 0)),
        pl.BlockSpec((None, s, CONV_W - 1, 2 * FFN_D), lambda i, j: (i, 0, 0, 0)),
    ]
    out_shape = [
        jax.ShapeDtypeStruct(x.shape, F32),
        jax.ShapeDtypeStruct(buf.shape, F32),
    ]
    cast_in, cast_out, cast_shape = _cast_specs(casts, grid)
    return pl.pallas_call(
        _with_casts(functools.partial(_ffn_body, s, t, final_norm, swap_out),
                    len(in_specs), len(out_specs), len(casts)),
        grid=grid,
        in_specs=in_specs + cast_in,
        out_specs=out_specs + cast_out,
        out_shape=out_shape + cast_shape,
        scratch_shapes=[] if s > 1 else [pltpu.VMEM((CONV_W - 1, 2 * FFN_D), F32)] + (
            [pltpu.VMEM((D_MODEL // LANES, tm, LANES), F32)] if swap_out else []),
        compiler_params=_params(FUSED_CALL_VMEM_MIB),
        name=name,
    )(x, mod, g, w_up, conv_w, conv_b, w_down, buf, final_g, *[w for w, _ in casts])


def kernel(x_prompt, x_sample, state_ret, state_ffn_conv, c_prompt, c_sample,
           w_ada, b_ada, norm_mix_g, norm_ffn_g,
           ret_w_in, ret_gn_g, ret_w_out,
           sgu_w_in, sgu_ln_g, sgu_ln_b, sgu_w_s, sgu_b_s, sgu_w_out,
           ffn_w_up, ffn_conv_w, ffn_conv_b, ffn_w_down, final_g):
    bp, seq, _ = x_prompt.shape
    bs, dec, _ = x_sample.shape
    depth = w_ada.shape[0]
    ss = SAMPLE_ROW_TILE // dec
    nbs = bs // ss

    mod = _ada(jnp.concatenate([c_prompt, c_sample], axis=0), w_ada, b_ada)
    mod_p = mod[:, :bp].reshape(depth, bp, 1, -1)
    mod_s = mod[:, bp:].reshape(depth, nbs, ss, -1)

    xp = x_prompt
    xs = x_sample.reshape(nbs, SAMPLE_ROW_TILE, D_MODEL)
    fg = final_g.reshape(1, D_MODEL)
    zero_buf = jnp.zeros((bp, 1, CONV_W - 1, 2 * FFN_D), F32)
    swap_seq = _swap_index(seq)
    swap_tile = _swap_index(ROW_TILE)

    def mixer_weights(layer):
        src = (ret_w_in, ret_w_out) if layer % 2 == 0 else (sgu_w_in, sgu_w_out)
        return [(w, layer // 2) for w in src]

    w_in, w_out = [_to_resident_bf16(w[j]) for w, j in mixer_weights(0)]

    ret_p, ret_s, conv_p, conv_s, sgu_s = [], [], [], [], []
    for i in range(depth):
        j = i // 2
        g_mix = norm_mix_g[i].reshape(1, D_MODEL)
        ffn_casts = [(ffn_w_up, i), (ffn_w_down, i)]
        if i % 2 == 0:
            gn_g = ret_gn_g[j].reshape(1, RET_V)
            cos_p, sin_p = [tab[swap_seq] for tab in _rope_tables(0, seq)]
            mask, qdec, kdec, gs = _decay_tables(ROW_TILE, RET_CHUNK)
            xp, sp, w_up, w_down = _ret_prompt(
                xp, mod_p[i], g_mix, w_in, cos_p, sin_p,
                mask[:, swap_tile][:, :, swap_tile], qdec[swap_tile], kdec[swap_tile], gs,
                gn_g, w_out, i == 0, casts=ffn_casts)
            cos_s, sin_s = _rope_tables(PAST_LEN, dec)
            proj = _proj_sample(xs, mod_s[i], g_mix, w_in, 2 * RET_QK + 2 * RET_V, ss, dec)
            gated, s_new = _ret_sample_core(
                proj.reshape(bs, dec, -1), state_ret[j], cos_s, sin_s,
                *_decay_tables(dec, min(dec, RET_CHUNK)), gn_g)
            xs = _out_sample(gated.reshape(nbs, SAMPLE_ROW_TILE, RET_V), xs, mod_s[i],
                             w_out, ss, dec)
            ret_p.append(sp)
            ret_s.append(s_new)
        else:
            ln_g = sgu_ln_g[j].reshape(1, SGU_D)
            ln_b = sgu_ln_b[j].reshape(1, SGU_D)
            wm_p, bm_p = _sgu_mix_tables(sgu_w_s[j], sgu_b_s[j], seq)
            wm_s, bm_s = _sgu_mix_tables(sgu_w_s[j], sgu_b_s[j], dec)
            xp, w_up, w_down = _sgu(
                xp, mod_p[i], g_mix, w_in, ln_g, ln_b,
                wm_p[:, swap_tile][:, :, swap_tile], bm_p[swap_tile], w_out,
                1, MIXER_ROW_TILE, False, casts=ffn_casts)
            xs, vs = _sgu(xs, mod_s[i], g_mix, w_in, ln_g, ln_b, wm_s, bm_s, w_out,
                          ss, dec, True)
            sgu_s.append(vs.reshape(bs, dec, SGU_D))
        g_ffn = norm_ffn_g[i].reshape(1, D_MODEL)
        conv_b = ffn_conv_b[i].reshape(1, 2 * FFN_D)
        last = i == depth - 1
        xp, cp, *next_mixer = _ffn(
            xp, mod_p[i], g_ffn, w_up, ffn_conv_w[i], conv_b, w_down, zero_buf, fg,
            1, FFN_ROW_TILE, last, last, "ffn_prompt_%d" % i,
            casts=[] if last else mixer_weights(i + 1))
        xs, cs = _ffn(xs, mod_s[i], g_ffn, w_up, ffn_conv_w[i], conv_b, w_down,
                      state_ffn_conv[i].reshape(nbs, ss, CONV_W - 1, 2 * FFN_D),
                      fg, ss, dec, last, False, "ffn_sample_%d" % i)
        if not last:
            w_in, w_out = next_mixer
        conv_p.append(cp.reshape(bp, CONV_W - 1, 2 * FFN_D))
        conv_s.append(cs.reshape(bs, CONV_W - 1, 2 * FFN_D))

    return (xp, xs.reshape(bs, dec, D_MODEL), jnp.stack(ret_p), jnp.stack(ret_s),
            jnp.stack(conv_p), jnp.stack(conv_s), jnp.stack(sgu_s))
```

```python
import functools

import jax
import jax.numpy as jnp
import numpy as np
from jax import lax
from jax.experimental import pallas as pl
from jax.experimental.pallas import tpu as pltpu

D_MODEL = 1024
N_MOD = 6
RET_HEADS = 4
RET_DK = 256
RET_DV = 512
RET_QK = RET_HEADS * RET_DK
RET_V = RET_HEADS * RET_DV
RET_CHUNK = 64
ROPE_BASE = 10000.0
PAST_LEN = 1024
SGU_CHUNK = 128
SGU_GROUPS = 4
SGU_D = 3 * D_MODEL
SGU_DG = SGU_D // SGU_GROUPS
FFN_D = 2816
CONV_W = 3
EPS = 1e-6

ROW_TILE = 256
FFN_ROW_TILE = 512
MIXER_ROW_TILE = 512
SAMPLE_ROW_TILE = 512
RET_SAMPLE_STREAMS = 4
FFN_CHUNKS = ((0, 768), (768, 768), (1536, 768), (2304, 512))
MIB = 1024 * 1024
FUSED_CALL_VMEM_MIB = 56
SMALL_CALL_VMEM_MIB = 48
LIGHT_CALL_VMEM_MIB = 32
LANES = 128
SUBLANES = 8
ROW_GROUP = SUBLANES * SUBLANES
LOG2_E = float(np.log2(np.e))
BF16_SUBLANES = 16

BF16 = jnp.bfloat16
F32 = jnp.float32


def _dot(a, b):
    return jnp.dot(a, b, preferred_element_type=F32)


def _resident(shape):
    nd = len(shape)
    return pl.BlockSpec(shape, lambda *_: (0,) * nd, pipeline_mode=pl.Buffered(1))


def _padded_cols(cols):
    return cols + LANES if cols % (8 * LANES) == 0 else cols


def _to_resident_bf16(w):
    rows, cols = w.shape
    return jnp.pad(w.astype(BF16), ((0, 0), (0, _padded_cols(cols) - cols)))


def _cast_specs(jobs, grid):
    steps = grid[0] * grid[1]
    in_specs, out_specs, out_shape = [], [], []
    for w, layer in jobs:
        _, rows, cols = w.shape
        parts = max(p for p in range(1, steps + 1)
                    if rows % p == 0 and (rows // p) % BF16_SUBLANES == 0)

        def slab(i, j, parts=parts):
            return jnp.minimum(i * grid[1] + j, parts - 1)

        in_specs.append(pl.BlockSpec((None, rows // parts, cols),
                                     lambda i, j, layer=layer, slab=slab: (layer, slab(i, j), 0)))
        out_specs.append(pl.BlockSpec((rows // parts, _padded_cols(cols)),
                                      lambda i, j, slab=slab: (slab(i, j), 0)))
        out_shape.append(jax.ShapeDtypeStruct((rows, _padded_cols(cols)), BF16))
    return in_specs, out_specs, out_shape


def _with_casts(body, n_in, n_out, n_cast):
    def wrapped(*refs):
        ins, refs = refs[:n_in], refs[n_in:]
        cast_in, refs = refs[:n_cast], refs[n_cast:]
        outs, refs = refs[:n_out], refs[n_out:]
        cast_out, scratch = refs[:n_cast], refs[n_cast:]
        for src, dst in zip(cast_in, cast_out):
            rows, cols = src.shape
            dst[:, :cols] = src[...].astype(BF16)
            if dst.shape[1] > cols:
                dst[:, cols:] = jnp.zeros((rows, dst.shape[1] - cols), BF16)
        body(*ins, *outs, *scratch)
    return wrapped


def _params(vmem_mib):
    return pltpu.CompilerParams(
        dimension_semantics=("arbitrary", "arbitrary"),
        vmem_limit_bytes=vmem_mib * MIB)


def _rows(v, s, t):
    if s == 1:
        return v
    return jnp.broadcast_to(v[:, None, :], (s, t, v.shape[-1])).reshape(s * t, v.shape[-1])


def _rmsnorm(x, g):
    ms = jnp.mean(x * x, axis=-1, keepdims=True)
    return x * lax.rsqrt(ms + EPS) * g


def _modulated(x, g, shift, scale, s, t):
    ms = jnp.mean(x * x, axis=-1, keepdims=True)
    return x * lax.rsqrt(ms + EPS) * _rows(g * (1.0 + scale), s, t) + _rows(shift, s, t)


def _silu(x):
    return x / (1.0 + jnp.exp2(x * -LOG2_E))


def _gelu(x):
    k = -2.0 * (2.0 / np.pi) ** 0.5 * LOG2_E
    return x / (1.0 + jnp.exp2(x * (k + (k * 0.044715) * (x * x))))


def _layernorm_nogain(o):
    mu = jnp.mean(o, axis=-1, keepdims=True)
    d = o - mu
    var = jnp.mean(d * d, axis=-1, keepdims=True)
    return d * lax.rsqrt(var + EPS)


def _swap_rows(val, scr):
    rows, cols = val.shape
    per = ROW_GROUP // SUBLANES
    lanes = []
    for lane in range(cols // LANES):
        scr[lane] = val[:, LANES * lane:LANES * (lane + 1)]
        tiles = []
        for k in range(rows // SUBLANES):
            grp, v = divmod(k, per)
            tiles.append(scr.at[lane][pl.ds(ROW_GROUP * grp + v, SUBLANES, stride=per), :])
        lanes.append(jnp.concatenate(tiles, axis=0))
    return jnp.concatenate(lanes, axis=-1)


def _swap_index(n):
    per = ROW_GROUP // SUBLANES
    return np.arange(n).reshape(n // ROW_GROUP, SUBLANES, per).transpose(0, 2, 1).reshape(n)


def _ada_body(c_ref, w_ref, b_ref, o_ref):
    a = _silu(c_ref[...]).astype(BF16)
    o_ref[...] = _dot(a, w_ref[...].astype(BF16)) + b_ref[...]


def _ada(c_all, w_ada, b_ada):
    depth = w_ada.shape[0]
    n = c_all.shape[0]
    return pl.pallas_call(
        _ada_body,
        grid=(depth, N_MOD),
        in_specs=[
            pl.BlockSpec((n, D_MODEL), lambda i, j: (0, 0)),
            pl.BlockSpec((None, D_MODEL, D_MODEL), lambda i, j: (i, 0, j)),
            pl.BlockSpec((None, 1, D_MODEL), lambda i, j: (i, 0, j)),
        ],
        out_specs=pl.BlockSpec((None, n, D_MODEL), lambda i, j: (i, 0, j)),
        out_shape=jax.ShapeDtypeStruct((depth, n, N_MOD * D_MODEL), F32),
        compiler_params=_params(LIGHT_CALL_VMEM_MIB),
        name="ada",
    )(c_all, w_ada, b_ada.reshape(depth, 1, N_MOD * D_MODEL))


def _rotary(x, cos, sin):
    half = x.shape[-1] // 2
    x1, x2 = x[:, :half], x[:, half:]
    return jnp.concatenate([x1 * cos - x2 * sin, x1 * sin + x2 * cos], axis=-1)


def _ret_head(q, k, v, s_prev, cos, sin, mask, qdec, kdec, gs):
    qb = (_rotary(q, cos, sin) * (RET_DK ** -0.5)).astype(BF16)
    kr = _rotary(k, cos, sin)
    kb = kr.astype(BF16)
    vb = v.astype(BF16)
    scores = lax.dot_general(qb, kb, (((1,), (1,)), ((), ())),
                             preferred_element_type=F32) * mask
    o = _dot(scores.astype(BF16), vb) + qdec * _dot(qb, s_prev.astype(BF16))
    kd = (kr * kdec).T.astype(BF16)
    s_new = gs * s_prev + _dot(kd, vb)
    return o, s_new


def _ret_prompt_body(x_ref, mod_ref, g_ref, win_ref, cos_ref, sin_ref, mask_ref,
                     qdec_ref, kdec_ref, gs_ref, gng_ref, wout_ref,
                     xo_ref, so_ref, s_ref, *maybe_swap_ref):
    t = pl.program_id(1)

    @pl.when(t == 0)
    def _():
        s_ref[...] = jnp.zeros_like(s_ref)

    x = x_ref[...]
    if maybe_swap_ref:
        x = _swap_rows(x, maybe_swap_ref[0])
    mod = mod_ref[...]
    tm = x.shape[0]
    h = _modulated(x, g_ref[...], mod[:, 0:D_MODEL], mod[:, D_MODEL:2 * D_MODEL],
                   1, tm).astype(BF16)
    y = jnp.zeros((tm, D_MODEL), F32)
    for hd in range(RET_HEADS):
        q = _dot(h, win_ref[:, hd * RET_DK:(hd + 1) * RET_DK])
        k = _dot(h, win_ref[:, RET_QK + hd * RET_DK:RET_QK + (hd + 1) * RET_DK])
        v = _dot(h, win_ref[:, 2 * RET_QK + hd * RET_DV:2 * RET_QK + (hd + 1) * RET_DV])
        gate = _dot(h, win_ref[:, 2 * RET_QK + RET_V + hd * RET_DV:
                               2 * RET_QK + RET_V + (hd + 1) * RET_DV])
        state = s_ref[hd]
        blocks = []
        for r in range(0, tm, ROW_TILE):
            rows = slice(r, r + ROW_TILE)
            o, state = _ret_head(q[rows], k[rows], v[rows], state,
                                 cos_ref[rows, :], sin_ref[rows, :], mask_ref[hd],
                                 qdec_ref[:, hd:hd + 1], kdec_ref[:, hd:hd + 1],
                                 gs_ref[:, hd:hd + 1])
            blocks.append(o)
        s_ref[hd] = state
        o = blocks[0] if len(blocks) == 1 else jnp.concatenate(blocks, axis=0)
        on = _layernorm_nogain(o) * gng_ref[:, hd * RET_DV:(hd + 1) * RET_DV]
        gated = (_silu(gate) * on).astype(BF16)
        y = y + _dot(gated, wout_ref[hd * RET_DV:(hd + 1) * RET_DV, :D_MODEL])
    xo_ref[...] = x + mod[:, 2 * D_MODEL:3 * D_MODEL] * y

    @pl.when(t == pl.num_programs(1) - 1)
    def _():
        so_ref[...] = s_ref[...]


def _ret_prompt(x, mod, g, w_in, cos, sin, mask, qdec, kdec, gs, gn_g, w_out, swap_in,
                casts=()):
    b, seq, _ = x.shape
    tm = MIXER_ROW_TILE
    grid = (b, seq // tm)
    in_specs = [
        pl.BlockSpec((None, tm, D_MODEL), lambda i, t: (i, t, 0)),
        pl.BlockSpec((None, 1, N_MOD * D_MODEL), lambda i, t: (i, 0, 0)),
        _resident((1, D_MODEL)),
        _resident(w_in.shape),
        pl.BlockSpec((tm, RET_DK // 2), lambda i, t: (t, 0)),
        pl.BlockSpec((tm, RET_DK // 2), lambda i, t: (t, 0)),
        _resident(mask.shape),
        _resident(qdec.shape),
        _resident(kdec.shape),
        _resident(gs.shape),
        _resident((1, RET_V)),
        _resident(w_out.shape),
    ]
    out_specs = [
        pl.BlockSpec((None, tm, D_MODEL), lambda i, t: (i, t, 0)),
        pl.BlockSpec((None, RET_HEADS, RET_DK, RET_DV), lambda i, t: (i, 0, 0, 0)),
    ]
    out_shape = [
        jax.ShapeDtypeStruct(x.shape, F32),
        jax.ShapeDtypeStruct((b, RET_HEADS, RET_DK, RET_DV), F32),
    ]
    cast_in, cast_out, cast_shape = _cast_specs(casts, grid)
    return pl.pallas_call(
        _with_casts(_ret_prompt_body, len(in_specs), len(out_specs), len(casts)),
        grid=grid,
        in_specs=in_specs + cast_in,
        out_specs=out_specs + cast_out,
        out_shape=out_shape + cast_shape,
        scratch_shapes=[pltpu.VMEM((RET_HEADS, RET_DK, RET_DV), F32)] + (
            [pltpu.VMEM((D_MODEL // LANES, tm, LANES), F32)] if swap_in else []),
        compiler_params=_params(FUSED_CALL_VMEM_MIB),
        name="ret_prompt",
    )(x, mod.reshape(b, 1, -1), g, w_in, cos, sin, mask, qdec, kdec, gs, gn_g, w_out,
      *[w for w, _ in casts])


def _proj_body(s, t, x_ref, mod_ref, g_ref, w_ref, o_ref):
    mod = mod_ref[...]
    h = _modulated(x_ref[...], g_ref[...], mod[:, 0:D_MODEL],
                   mod[:, D_MODEL:2 * D_MODEL], s, t).astype(BF16)
    o_ref[...] = _dot(h, w_ref[:, :o_ref.shape[-1]])


def _proj_sample(x, mod, g, w, n, s, t):
    nb, tm, _ = x.shape
    return pl.pallas_call(
        functools.partial(_proj_body, s, t),
        grid=(nb, 1),
        in_specs=[
            pl.BlockSpec((None, tm, D_MODEL), lambda i, j: (i, 0, 0)),
            pl.BlockSpec((None, s, N_MOD * D_MODEL), lambda i, j: (i, 0, 0)),
            _resident((1, D_MODEL)),
            _resident(w.shape),
        ],
        out_specs=pl.BlockSpec((None, tm, n), lambda i, j: (i, 0, 0)),
        out_shape=jax.ShapeDtypeStruct((nb, tm, n), F32),
        compiler_params=_params(SMALL_CALL_VMEM_MIB),
        name="ret_sample_proj",
    )(x, mod, g, w)


def _ret_sample_core_body(p_ref, s0_ref, cos_ref, sin_ref, mask_ref, qdec_ref,
                          kdec_ref, gs_ref, gng_ref, o_ref, so_ref):
    cos = cos_ref[...]
    sin = sin_ref[...]
    for b in range(RET_SAMPLE_STREAMS):
        for hd in range(RET_HEADS):
            q = p_ref[b, :, hd * RET_DK:(hd + 1) * RET_DK]
            k = p_ref[b, :, RET_QK + hd * RET_DK:RET_QK + (hd + 1) * RET_DK]
            v = p_ref[b, :, 2 * RET_QK + hd * RET_DV:2 * RET_QK + (hd + 1) * RET_DV]
            gate = p_ref[b, :, 2 * RET_QK + RET_V + hd * RET_DV:
                         2 * RET_QK + RET_V + (hd + 1) * RET_DV]
            o, s_new = _ret_head(q, k, v, s0_ref[b, hd], cos, sin, mask_ref[hd],
                                 qdec_ref[:, hd:hd + 1], kdec_ref[:, hd:hd + 1],
                                 gs_ref[:, hd:hd + 1])
            so_ref[b, hd] = s_new
            on = _layernorm_nogain(o) * gng_ref[:, hd * RET_DV:(hd + 1) * RET_DV]
            o_ref[b, :, hd * RET_DV:(hd + 1) * RET_DV] = _silu(gate) * on


def _ret_sample_core(proj, s0, cos, sin, mask, qdec, kdec, gs, gn_g):
    b, t, n = proj.shape
    nb = RET_SAMPLE_STREAMS
    return pl.pallas_call(
        _ret_sample_core_body,
        grid=(b // nb, 1),
        in_specs=[
            pl.BlockSpec((nb, t, n), lambda i, j: (i, 0, 0)),
            pl.BlockSpec((nb, RET_HEADS, RET_DK, RET_DV), lambda i, j: (i, 0, 0, 0)),
            _resident(cos.shape),
            _resident(sin.shape),
            _resident(mask.shape),
            _resident(qdec.shape),
            _resident(kdec.shape),
            _resident(gs.shape),
            _resident((1, RET_V)),
        ],
        out_specs=[
            pl.BlockSpec((nb, t, RET_V), lambda i, j: (i, 0, 0)),
            pl.BlockSpec((nb, RET_HEADS, RET_DK, RET_DV), lambda i, j: (i, 0, 0, 0)),
        ],
        out_shape=[
            jax.ShapeDtypeStruct((b, t, RET_V), F32),
            jax.ShapeDtypeStruct(s0.shape, F32),
        ],
        compiler_params=_params(SMALL_CALL_VMEM_MIB),
        name="ret_sample_core",
    )(proj, s0, cos, sin, mask, qdec, kdec, gs, gn_g)


def _out_body(s, t, a_ref, x_ref, mod_ref, w_ref, o_ref):
    y = _dot(a_ref[...].astype(BF16), w_ref[:, :D_MODEL])
    gate = _rows(mod_ref[...][:, 2 * D_MODEL:3 * D_MODEL], s, t)
    o_ref[...] = x_ref[...] + gate * y


def _out_sample(a, x, mod, w, s, t):
    nb, tm, k = a.shape
    return pl.pallas_call(
        functools.partial(_out_body, s, t),
        grid=(nb, 1),
        in_specs=[
            pl.BlockSpec((None, tm, k), lambda i, j: (i, 0, 0)),
            pl.BlockSpec((None, tm, D_MODEL), lambda i, j: (i, 0, 0)),
            pl.BlockSpec((None, s, N_MOD * D_MODEL), lambda i, j: (i, 0, 0)),
            _resident(w.shape),
        ],
        out_specs=pl.BlockSpec((None, tm, D_MODEL), lambda i, j: (i, 0, 0)),
        out_shape=jax.ShapeDtypeStruct(x.shape, F32),
        compiler_params=_params(LIGHT_CALL_VMEM_MIB),
        name="ret_sample_out",
    )(a, x, mod, w)


def _decay_tables(block, chunk):
    log_gamma = np.log(1.0 - np.exp2(-5.0 - np.arange(RET_HEADS)))
    idx = np.arange(block, dtype=np.float64)
    ch = np.arange(block) // chunk
    dist = np.abs(idx[:, None] - idx[None, :])
    decay = np.exp(log_gamma[:, None, None] * dist[None])
    mask = np.where((ch[None, :] <= ch[:, None])[None], decay, 0.0)
    qdec = np.exp((idx[:, None] + 1.0) * log_gamma[None, :])
    kdec = np.exp((block - 1.0 - idx)[:, None] * log_gamma[None, :])
    gs = np.exp(block * log_gamma)[None, :]
    return tuple(jnp.asarray(a, F32) for a in (mask, qdec, kdec, gs))


def _rope_tables(first, count):
    half = RET_DK // 2
    inv = np.power(ROPE_BASE, -np.arange(half, dtype=np.float64) / half)
    ang = (first + np.arange(count, dtype=np.float64))[:, None] * inv[None, :]
    return jnp.asarray(np.cos(ang), F32), jnp.asarray(np.sin(ang), F32)


def _sgu_body(s, t, emit_v, x_ref, mod_ref, g_ref, win_ref, lng_ref, lnb_ref,
              wmix_ref, bmix_ref, wout_ref, xo_ref, *maybe_v_ref):
    x = x_ref[...]
    mod = mod_ref[...]
    h = _modulated(x, g_ref[...], mod[:, 0:D_MODEL], mod[:, D_MODEL:2 * D_MODEL],
                   s, t).astype(BF16)
    groups = range(SGU_GROUPS)

    def cols(grp, base):
        return slice(base + grp * SGU_DG, base + (grp + 1) * SGU_DG)

    def proj(grp, base):
        return _dot(h, win_ref[:, cols(grp, base)])

    pv = [proj(0, SGU_D)]
    v = []
    for grp in groups:
        pv.append(proj(grp + 1, SGU_D) if grp + 1 < SGU_GROUPS else proj(0, 0))
        v.append(_gelu(pv[grp]))
    pu = [pv.pop(), proj(1, 0)]
    mu = sum(jnp.sum(vg, axis=-1, keepdims=True) for vg in v) * (1.0 / SGU_D)
    d = [vg - mu for vg in v]
    var = sum(jnp.sum(dg * dg, axis=-1, keepdims=True) for dg in d) * (1.0 / SGU_D)
    rs = lax.rsqrt(var + EPS)
    vb = []
    for grp in groups:
        vn = d[grp] * rs * lng_ref[:, cols(grp, 0)] + lnb_ref[:, cols(grp, 0)]
        if emit_v:
            maybe_v_ref[0][:, cols(grp, 0)] = vn
        vb.append(vn.astype(BF16))
    def mix(grp):
        parts = [_dot(wmix_ref[grp], vb[grp][r:r + ROW_TILE]) + bmix_ref[:, grp:grp + 1]
                 for r in range(0, s * t, ROW_TILE)]
        return parts[0] if len(parts) == 1 else jnp.concatenate(parts, axis=0)

    y = jnp.zeros((s * t, D_MODEL), F32)
    mixed = mix(0)
    for grp in groups:
        if grp + 2 < SGU_GROUPS:
            pu.append(proj(grp + 2, 0))
        u = _gelu(pu[grp])
        cur = mixed
        if grp + 1 < SGU_GROUPS:
            mixed = mix(grp + 1)
        gated = (u * cur).astype(BF16)
        y = y + _dot(gated, wout_ref[cols(grp, 0), :D_MODEL])
    xo_ref[...] = x + _rows(mod[:, 2 * D_MODEL:3 * D_MODEL], s, t) * y


def _sgu(x, mod, g, w_in, ln_g, ln_b, w_mix, b_mix, w_out, s, t, emit_v, casts=()):
    nb, r, _ = x.shape
    tm = s * t
    grid = (nb, r // tm)
    in_specs = [
        pl.BlockSpec((None, tm, D_MODEL), lambda i, j: (i, j, 0)),
        pl.BlockSpec((None, s, N_MOD * D_MODEL), lambda i, j: (i, 0, 0)),
        _resident((1, D_MODEL)),
        _resident(w_in.shape),
        _resident((1, SGU_D)),
        _resident((1, SGU_D)),
        _resident(w_mix.shape),
        _resident(b_mix.shape),
        _resident(w_out.shape),
    ]
    out_specs = [pl.BlockSpec((None, tm, D_MODEL), lambda i, j: (i, j, 0))]
    out_shape = [jax.ShapeDtypeStruct(x.shape, F32)]
    if emit_v:
        out_specs.append(pl.BlockSpec((None, tm, SGU_D), lambda i, j: (i, j, 0)))
        out_shape.append(jax.ShapeDtypeStruct((nb, r, SGU_D), F32))
    cast_in, cast_out, cast_shape = _cast_specs(casts, grid)
    return pl.pallas_call(
        _with_casts(functools.partial(_sgu_body, s, t, emit_v),
                    len(in_specs), len(out_specs), len(casts)),
        grid=grid,
        in_specs=in_specs + cast_in,
        out_specs=out_specs + cast_out,
        out_shape=out_shape + cast_shape,
        compiler_params=_params(FUSED_CALL_VMEM_MIB),
        name="sgu_sample" if emit_v else "sgu_prompt",
    )(x, mod, g, w_in, ln_g, ln_b, w_mix, b_mix, w_out, *[w for w, _ in casts])


def _sgu_mix_tables(w_s, b_s, seq_len):
    length = min(seq_len, SGU_CHUNK)
    blk = np.arange(SGU_CHUNK) // RET_CHUNK
    w = jnp.where(blk[None, :] <= blk[:, None], w_s, 0)[:, :length, :length]
    reps = ROW_TILE // length
    seg = np.arange(ROW_TILE) // length
    tiled = jnp.broadcast_to(w[:, None, :, None, :], (SGU_GROUPS, reps, length, reps, length))
    tile = jnp.where(seg[None, :] == seg[:, None], tiled.reshape(SGU_GROUPS, ROW_TILE, ROW_TILE), 0)
    bias = jnp.tile(b_s[:, :length].T, (reps, 1))
    return tile.astype(BF16), bias


def _ffn_body(s, t, final_norm, swap_out, x_ref, mod_ref, g_ref, wup_ref, cw_ref, cb_ref,
              wdn_ref, buf_ref, fg_ref, xo_ref, co_ref, *scratch):
    tm = s * t
    x = x_ref[...]
    mod = mod_ref[...]
    h = _modulated(x, g_ref[...], mod[:, 3 * D_MODEL:4 * D_MODEL],
                   mod[:, 4 * D_MODEL:5 * D_MODEL], s, t).astype(BF16)
    if s == 1:
        carry_ref = scratch[0]

        @pl.when(pl.program_id(1) == 0)
        def _():
            carry_ref[...] = buf_ref[0]

    def up(c, w):
        return [_dot(h, wup_ref[:, base + c:base + c + w]) for base in (0, FFN_D)]

    def shifted_swapped(a, lo, w):
        groups = tm // ROW_GROUP
        per = ROW_GROUP // SUBLANES
        a4 = a.reshape(groups, per, SUBLANES, w)
        p0 = carry_ref[0:1, lo:lo + w]
        p1 = carry_ref[1:2, lo:lo + w]
        carry_ref[0:1, lo:lo + w] = a4[groups - 1, per - 2, SUBLANES - 1:, :]
        carry_ref[1:2, lo:lo + w] = a4[groups - 1, per - 1, SUBLANES - 1:, :]
        sub = lax.broadcasted_iota(jnp.int32, (1, SUBLANES, 1), 1)

        def wrapped(col, before):
            r = pltpu.roll(col, 1, 1)
            prev = jnp.concatenate(
                [jnp.broadcast_to(before[None], (1, SUBLANES, w)), r[:-1]], axis=0)
            return jnp.where(sub == 0, prev, r)

        w1 = wrapped(a4[:, per - 1], p1)[:, None]
        w2 = wrapped(a4[:, per - 2], p0)[:, None]
        a1 = jnp.concatenate([w1, a4[:, :per - 1]], axis=1).reshape(tm, w)
        a2 = jnp.concatenate([w2, w1, a4[:, :per - 2]], axis=1).reshape(tm, w)
        return a1, a2

    def shifted_streams(a, lo, w):
        r1 = pltpu.roll(a, 1, 0)
        r2 = pltpu.roll(a, 2, 0)
        p0 = _rows(buf_ref[:, 0, lo:lo + w], s, t)
        p1 = _rows(buf_ref[:, 1, lo:lo + w], s, t)
        co_ref[:, :, lo:lo + w] = a.reshape(s, t, w)[:, t - 2:, :]
        pos = lax.broadcasted_iota(jnp.int32, (tm, 1), 0) & (t - 1)
        a1 = jnp.where(pos == 0, p1, r1)
        a2 = jnp.where(pos == 0, p0, jnp.where(pos == 1, p1, r2))
        return a1, a2

    def conv(a, lo, w):
        a1, a2 = (shifted_swapped if s == 1 else shifted_streams)(a, lo, w)
        c = cb_ref[:, lo:lo + w] + a2 * cw_ref[0:1, lo:lo + w]
        c = c + a1 * cw_ref[1:2, lo:lo + w]
        return c + a * cw_ref[2:3, lo:lo + w]

    y = jnp.zeros((tm, D_MODEL), F32)
    nxt = up(*FFN_CHUNKS[0])
    for k, (c, w) in enumerate(FFN_CHUNKS):
        cur = nxt
        if k + 1 < len(FFN_CHUNKS):
            nxt = up(*FFN_CHUNKS[k + 1])
        gate = conv(cur[0], c, w)
        val = conv(cur[1], FFN_D + c, w)
        hid = (_silu(gate) * val).astype(BF16)
        y = y + _dot(hid, wdn_ref[c:c + w, :D_MODEL])
    if s == 1:
        co_ref[0] = carry_ref[...]
    xn = x + _rows(mod[:, 5 * D_MODEL:6 * D_MODEL], s, t) * y
    if final_norm:
        xn = _rmsnorm(xn, fg_ref[...])
    if swap_out:
        xn = _swap_rows(xn, scratch[1])
    xo_ref[...] = xn


def _ffn(x, mod, g, w_up, conv_w, conv_b, w_down, buf, final_g, s, t, final_norm, swap_out,
         name, casts=()):
    nb, r, _ = x.shape
    tm = s * t
    assert t & (t - 1) == 0 and t >= CONV_W - 1 and (s > 1 or t % ROW_GROUP == 0)
    grid = (nb, r // tm)
    in_specs = [
        pl.BlockSpec((None, tm, D_MODEL), lambda i, j: (i, j, 0)),
        pl.BlockSpec((None, s, N_MOD * D_MODEL), lambda i, j: (i, 0, 0)),
        _resident((1, D_MODEL)),
        _resident(w_up.shape),
        _resident((CONV_W, 2 * FFN_D)),
        _resident((1, 2 * FFN_D)),
        _resident(w_down.shape),
        pl.BlockSpec((None, s, CONV_W - 1, 2 * FFN_D), lambda i, j: (i, 0, 0, 0)),
        _resident((1, D_MODEL)),
    ]
    out_specs = [
        pl.BlockSpec((None, tm, D_MODEL), lambda i, j: (i, j, 0)),
        pl.BlockSpec((None, s, CONV_W - 1, 2 * FFN_D), lambda i, j: (i, 0, 0, 0)),
    ]
    out_shape = [
        jax.ShapeDtypeStruct(x.shape, F32),
        jax.ShapeDtypeStruct(buf.shape, F32),
    ]
    cast_in, cast_out, cast_shape = _cast_specs(casts, grid)
    return pl.pallas_call(
        _with_casts(functools.partial(_ffn_body, s, t, final_norm, swap_out),
                    len(in_specs), len(out_specs), len(casts)),
        grid=grid,
        in_specs=in_specs + cast_in,
        out_specs=out_specs + cast_out,
        out_shape=out_shape + cast_shape,
        scratch_shapes=[] if s > 1 else [pltpu.VMEM((CONV_W - 1, 2 * FFN_D), F32)] + (
            [pltpu.VMEM((D_MODEL // LANES, tm, LANES), F32)] if swap_out else []),
        compiler_params=_params(FUSED_CALL_VMEM_MIB),
        name=name,
    )(x, mod, g, w_up, conv_w, conv_b, w_down, buf, final_g, *[w for w, _ in casts])


def kernel(x_prompt, x_sample, state_ret, state_ffn_conv, c_prompt, c_sample,
           w_ada, b_ada, norm_mix_g, norm_ffn_g,
           ret_w_in, ret_gn_g, ret_w_out,
           sgu_w_in, sgu_ln_g, sgu_ln_b, sgu_w_s, sgu_b_s, sgu_w_out,
           ffn_w_up, ffn_conv_w, ffn_conv_b, ffn_w_down, final_g):
    bp, seq, _ = x_prompt.shape
    bs, dec, _ = x_sample.shape
    depth = w_ada.shape[0]
    ss = SAMPLE_ROW_TILE // dec
    nbs = bs // ss

    mod = _ada(jnp.concatenate([c_prompt, c_sample], axis=0), w_ada, b_ada)
    mod_p = mod[:, :bp].reshape(depth, bp, 1, -1)
    mod_s = mod[:, bp:].reshape(depth, nbs, ss, -1)

    xp = x_prompt
    xs = x_sample.reshape(nbs, SAMPLE_ROW_TILE, D_MODEL)
    fg = final_g.reshape(1, D_MODEL)
    zero_buf = jnp.zeros((bp, 1, CONV_W - 1, 2 * FFN_D), F32)
    swap_seq = _swap_index(seq)
    swap_tile = _swap_index(ROW_TILE)

    def mixer_weights(layer):
        src = (ret_w_in, ret_w_out) if layer % 2 == 0 else (sgu_w_in, sgu_w_out)
        return [(w, layer // 2) for w in src]

    w_in, w_out = [_to_resident_bf16(w[j]) for w, j in mixer_weights(0)]

    ret_p, ret_s, conv_p, conv_s, sgu_s = [], [], [], [], []
    for i in range(depth):
        j = i // 2
        g_mix = norm_mix_g[i].reshape(1, D_MODEL)
        ffn_casts = [(ffn_w_up, i), (ffn_w_down, i)]
        if i % 2 == 0:
            gn_g = ret_gn_g[j].reshape(1, RET_V)
            cos_p, sin_p = [tab[swap_seq] for tab in _rope_tables(0, seq)]
            mask, qdec, kdec, gs = _decay_tables(ROW_TILE, RET_CHUNK)
            xp, sp, w_up, w_down = _ret_prompt(
                xp, mod_p[i], g_mix, w_in, cos_p, sin_p,
                mask[:, swap_tile][:, :, swap_tile], qdec[swap_tile], kdec[swap_tile], gs,
                gn_g, w_out, i == 0, casts=ffn_casts)
            cos_s, sin_s = _rope_tables(PAST_LEN, dec)
            proj = _proj_sample(xs, mod_s[i], g_mix, w_in, 2 * RET_QK + 2 * RET_V, ss, dec)
            gated, s_new = _ret_sample_core(
                proj.reshape(bs, dec, -1), state_ret[j], cos_s, sin_s,
                *_decay_tables(dec, min(dec, RET_CHUNK)), gn_g)
            xs = _out_sample(gated.reshape(nbs, SAMPLE_ROW_TILE, RET_V), xs, mod_s[i],
                             w_out, ss, dec)
            ret_p.append(sp)
            ret_s.append(s_new)
        else:
            ln_g = sgu_ln_g[j].reshape(1, SGU_D)
            ln_b = sgu_ln_b[j].reshape(1, SGU_D)
            wm_p, bm_p = _sgu_mix_tables(sgu_w_s[j], sgu_b_s[j], seq)
            wm_s, bm_s = _sgu_mix_tables(sgu_w_s[j], sgu_b_s[j], dec)
            xp, w_up, w_down = _sgu(
                xp, mod_p[i], g_mix, w_in, ln_g, ln_b,
                wm_p[:, swap_tile][:, :, swap_tile], bm_p[swap_tile], w_out,
                1, MIXER_ROW_TILE, False, casts=ffn_casts)
            xs, vs = _sgu(xs, mod_s[i], g_mix, w_in, ln_g, ln_b, wm_s, bm_s, w_out,
                          ss, dec, True)
            sgu_s.append(vs.reshape(bs, dec, SGU_D))
        g_ffn = norm_ffn_g[i].reshape(1, D_MODEL)
        conv_b = ffn_conv_b[i].reshape(1, 2 * FFN_D)
        last = i == depth - 1
        xp, cp, *next_mixer = _ffn(
            xp, mod_p[i], g_ffn, w_up, ffn_conv_w[i], conv_b, w_down, zero_buf, fg,
            1, FFN_ROW_TILE, last, last, "ffn_prompt_%d" % i,
            casts=[] if last else mixer_weights(i + 1))
        xs, cs = _ffn(xs, mod_s[i], g_ffn, w_up, ffn_conv_w[i], conv_b, w_down,
                      state_ffn_conv[i].reshape(nbs, ss, CONV_W - 1, 2 * FFN_D),
                      fg, ss, dec, last, False, "ffn_sample_%d" % i)
        if not last:
            w_in, w_out = next_mixer
        conv_p.append(cp.reshape(bp, CONV_W - 1, 2 * FFN_D))
        conv_s.append(cs.reshape(bs, CONV_W - 1, 2 * FFN_D))

    return (xp, xs.reshape(bs, dec, D_MODEL), jnp.stack(ret_p), jnp.stack(ret_s),
            jnp.stack(conv_p), jnp.stack(conv_s), jnp.stack(sgu_s))
```
